```python
import jax
import jax.numpy as jnp
from jax import lax
import numpy as np

D_MODEL = 1024
BATCH = 4
SEQ = 4096
DEPTH = 4

N_GROUPS = 4
GROUP = D_MODEL // N_GROUPS
CHUNK = 64

HGRN_HEADS = 4
GLA_HEADS = 4
GLA_KEY = GROUP // (2 * GLA_HEADS)
GLA_GATE_RANK = 16
GLA_GATE_NORM = 16.0
RWKV_HEADS = 4
RWKV_HEAD = GROUP // RWKV_HEADS
RWKV_DECAY_RANK = 64
RWKV_A_RANK = 64
RWKV_V_RANK = 32
RWKV_GATE_RANK = 160
RWKV_LNX_EPS = 64e-5
RET_HEADS = 4
RET_KEY = GROUP // RET_HEADS
ROPE_BASE = 10000.0

HGRN_COLS = (GROUP, GROUP, GROUP, GROUP)
GLA_COLS = (GLA_HEADS * GLA_KEY, GLA_HEADS * GLA_KEY, GROUP, GLA_GATE_RANK, GROUP)
RWKV_COLS = (GROUP, GROUP, GROUP, RWKV_DECAY_RANK, RWKV_A_RANK, RWKV_GATE_RANK)
RET_COLS = (GROUP, GROUP, GROUP, GROUP)
MIXER_COLS = (sum(HGRN_COLS), sum(GLA_COLS), sum(RWKV_COLS), sum(RET_COLS))
IN_COLS = sum(MIXER_COLS)

FFN_DENSE = 2816
N_EXPERTS = 8
TOP_K = 2
FFN_EXPERT = 1408
N_DENSE = (DEPTH + 1) // 2
N_MOE = DEPTH // 2

DN_ALPHA = (2.0 * DEPTH) ** 0.25
DN_BETA = (8.0 * DEPTH) ** -0.25
LN_EPS = 1e-5
F32 = jnp.float32

kernel_name = 'hybrid_hgrn2_gla_rwkv7_retnet_moe'


def _split(p, sizes):
    return jnp.split(p, np.cumsum(sizes)[:-1].tolist(), axis=-1)


def _heads(t, n):
    return t.reshape(t.shape[:-1] + (n, t.shape[-1] // n))


def _merge(t):
    return t.reshape(t.shape[:-2] + (t.shape[-2] * t.shape[-1],))


def _layer_norm(x, g, b):
    x = x.astype(F32)
    mu = jnp.mean(x, -1, keepdims=True)
    var = jnp.mean(jnp.square(x - mu), -1, keepdims=True)
    return (x - mu) * lax.rsqrt(var + LN_EPS) * g + b


def _rms_norm(x, g, eps=1e-6):
    return x * lax.rsqrt(jnp.mean(jnp.square(x), -1, keepdims=True) + eps) * g


def _group_norm(x, eps):
    mu = jnp.mean(x, -1, keepdims=True)
    var = jnp.mean(jnp.square(x - mu), -1, keepdims=True)
    return (x - mu) * lax.rsqrt(var + eps)


def _to_chunks(t):
    b, s, h, d = t.shape
    return t.reshape(b, s // CHUNK, CHUNK, h, d).transpose(1, 0, 3, 2, 4)


def _from_chunks(t):
    n, b, h, c, d = t.shape
    return t.transpose(1, 0, 3, 2, 4).reshape(b, n * c, h, d)


def _chunked_gated_linear(q, k, v, log_f):
    b, _, h, kd = q.shape
    vd = v.shape[-1]
    qc, kc, vc, gc = (_to_chunks(t.astype(F32)) for t in (q, k, v, log_f))
    causal = jnp.tril(jnp.ones((CHUNK, CHUNK), dtype=bool))

    def step(state, inp):
        qi, ki, vi, gi = inp
        cum = jnp.cumsum(gi, axis=2)
        diff = cum[:, :, :, None, :] - cum[:, :, None, :, :]
        decay = jnp.exp(jnp.where(causal[:, :, None], diff, -jnp.inf))
        scores = jnp.einsum('bhik,bhjk,bhijk->bhij', qi, ki, decay)
        out = (jnp.einsum('bhij,bhjv->bhiv', scores, vi)
               + jnp.einsum('bhik,bhkv->bhiv', qi * jnp.exp(cum), state))
        cum_end = cum[:, :, -1:, :]
        state = (jnp.exp(cum_end[:, :, 0, :])[..., None] * state
                 + jnp.einsum('bhjk,bhjv->bhkv', ki * jnp.exp(cum_end - cum), vi))
        return state, out

    _, o = lax.scan(step, jnp.zeros((b, h, kd, vd), F32), (qc, kc, vc, gc))
    return _from_chunks(o)


def _retention_chunkwise(q, k, v, log_gamma):
    b, _, h, kd = q.shape
    vd = v.shape[-1]
    qc, kc, vc = (_to_chunks(t.astype(F32)) for t in (q, k, v))
    idx = jnp.arange(CHUNK, dtype=F32)
    rel = idx[:, None] - idx[None, :]
    intra = jnp.exp(jnp.where(rel >= 0, rel * log_gamma[:, None, None], -jnp.inf))
    q_dec = jnp.exp((idx + 1.0)[None, :] * log_gamma[:, None])
    k_dec = jnp.exp((CHUNK - 1.0 - idx)[None, :] * log_gamma[:, None])
    c_dec = jnp.exp(CHUNK * log_gamma)

    def step(state, inp):
        qi, ki, vi = inp
        scores = jnp.einsum('bhik,bhjk->bhij', qi, ki) * intra
        out = (jnp.einsum('bhij,bhjv->bhiv', scores, vi)
               + jnp.einsum('bhik,bhkv->bhiv', qi * q_dec[..., None], state))
        state = (c_dec[:, None, None] * state
                 + jnp.einsum('bhjk,bhjv->bhkv', ki * k_dec[..., None], vi))
        return state, out

    _, o = lax.scan(step, jnp.zeros((b, h, kd, vd), F32), (qc, kc, vc))
    return _from_chunks(o)


def _rotary(t, positions):
    half = t.shape[-1] // 2
    inv_freq = ROPE_BASE ** (-jnp.arange(half, dtype=F32) / half)
    ang = positions.astype(F32)[:, :, None, None] * inv_freq
    cos, sin = jnp.cos(ang), jnp.sin(ang)
    t1, t2 = t[..., :half], t[..., half:]
    return jnp.concatenate([t1 * cos - t2 * sin, t1 * sin + t2 * cos], axis=-1)


def _rwkv7_scan(r, w, k, v, kk, a):
    b, _, h, n = r.shape
    xs = tuple(jnp.moveaxis(t, 1, 0) for t in (r, w, k, v, kk, a))

    def step(state, inp):
        rt, wt, kt, vt, kkt, at = inp
        sa = jnp.einsum('bhvk,bhk->bhv', state, -kkt)
        state = (state * wt[:, :, None, :]
                 + sa[..., None] * (kkt * at)[:, :, None, :]
                 + vt[..., None] * kt[:, :, None, :])
        return state, jnp.einsum('bhvk,bhk->bhv', state, rt)

    _, o = lax.scan(step, jnp.zeros((b, h, n, n), F32), xs)
    return jnp.moveaxis(o, 0, 1)


def _hgrn2_mixer(p, lb, norm_g):
    q_raw, f_raw, i_raw, g_raw = _split(p, HGRN_COLS)
    f = lb + (1.0 - lb) * jax.nn.sigmoid(f_raw)
    hd = lambda t: _heads(t, HGRN_HEADS)
    o = _chunked_gated_linear(hd(jax.nn.silu(q_raw)), hd(1.0 - f), hd(i_raw), hd(jnp.log(f)))
    o = _rms_norm(o, norm_g) * jax.nn.silu(hd(g_raw))
    return _merge(o)


def _gla_mixer(p, gate_w2, gate_b, norm_g):
    q, k, v, g_lr, r = _split(p, GLA_COLS)
    log_f = jax.nn.log_sigmoid(g_lr @ gate_w2 + gate_b) / GLA_GATE_NORM
    hd = lambda t: _heads(t, GLA_HEADS)
    o = _chunked_gated_linear(hd(q) * GLA_KEY ** -0.5, hd(k), hd(v), hd(log_f))
    o = _rms_norm(o, norm_g) * jax.nn.silu(hd(r))
    return _merge(o)


def _rwkv7_mixer(p, v_first, v_mix, mu, w0, w2, a0, a2, g2, k_k, k_a, r_k, lnx_g, lnx_b):
    p_prev = jnp.pad(p, ((0, 0), (1, 0), (0, 0)))[:, :-1]
    p = p + (p_prev - p) * mu
    r, k, v, w_lr, a_lr, g_lr = _split(p, RWKV_COLS)
    w = -jax.nn.softplus(-(w0 + jnp.tanh(w_lr) @ w2)) - 0.5
    decay = jnp.exp(-jnp.exp(w))
    a = jax.nn.sigmoid(a0 + a_lr @ a2)
    g = jax.nn.sigmoid(g_lr) @ g2
    if v_mix is None:
        v_first = v
    else:
        v0, v1, v2 = v_mix
        v = v + (v_first - v) * jax.nn.sigmoid(v0 + (v @ v1) @ v2)
    hd = lambda t: _heads(t, RWKV_HEADS)
    kk = hd(k * k_k)
    kk = kk / jnp.maximum(jnp.sqrt(jnp.sum(jnp.square(kk), -1, keepdims=True)), 1e-12)
    k = k * (1.0 + (a - 1.0) * k_a)
    rh, kh, vh = hd(r), hd(k), hd(v)
    o = _rwkv7_scan(rh, hd(decay), kh, vh, kk, hd(a))
    o = _group_norm(o, RWKV_LNX_EPS) * hd(lnx_g) + hd(lnx_b)
    o = o + jnp.sum(rh * kh * r_k, -1, keepdims=True) * vh
    return _merge(o) * g, v_first


def _retnet_mixer(p, positions, log_gamma):
    q, k, v, g = _split(p, RET_COLS)
    hd = lambda t: _heads(t, RET_HEADS)
    qh = _rotary(hd(q), positions)
    kh = _rotary(hd(k), positions) * RET_KEY ** -0.5
    o = _group_norm(_retention_chunkwise(qh, kh, hd(v), log_gamma), LN_EPS)
    return jax.nn.silu(g) * _merge(o)


def _swiglu(h, wg, wu, wd):
    return (jax.nn.silu(h @ wg) * (h @ wu)) @ wd


def _moe(h, router, wg, wu, wd):
    b, s, d = h.shape
    hf = h.reshape(b * s, d)
    probs = jax.nn.softmax((hf @ router).astype(F32), axis=-1)
    top_p, top_i = lax.top_k(probs, TOP_K)
    top_p = top_p / jnp.sum(top_p, -1, keepdims=True)
    gates = jnp.sum(jax.nn.one_hot(top_i, N_EXPERTS, dtype=F32) * top_p[..., None], axis=1)
    out = jnp.zeros_like(hf)
    for e in range(N_EXPERTS):
        out = out + gates[:, e:e + 1] * _swiglu(hf, wg[e], wu[e], wd[e])
    return out.reshape(b, s, d)


def setup_inputs(seed: int = 0) -> dict:
    key = jax.random.key(seed)
    ks = iter(jax.random.split(key, 48))

    def nrm(shape, scale):
        return scale * jax.random.normal(next(ks), shape, F32)

    D = D_MODEL
    decay_base = -6.0 + 5.0 * (jnp.arange(GROUP, dtype=F32) / (GROUP - 1)) ** 0.9
    return {
        'x': nrm((BATCH, SEQ, D), 1.0),
        'positions': jnp.broadcast_to(jnp.arange(SEQ, dtype=jnp.int32)[None, :], (BATCH, SEQ)),
        'w_in': nrm((DEPTH, D, IN_COLS), D ** -0.5),
        'w_out': nrm((DEPTH, D, D), DN_BETA * D ** -0.5),
        'ln1_g': 1.0 + nrm((DEPTH, D), 0.01),
        'ln1_b': nrm((DEPTH, D), 0.01),
        'ln2_g': 1.0 + nrm((DEPTH, D), 0.01),
        'ln2_b': nrm((DEPTH, D), 0.01),
        'hgrn_lb_logits': nrm((DEPTH, GROUP), 0.5),
        'hgrn_norm_g': 1.0 + nrm((DEPTH, GROUP // HGRN_HEADS), 0.01),
        'gla_gate_w2': nrm((DEPTH, GLA_GATE_RANK, GLA_HEADS * GLA_KEY), GLA_GATE_RANK ** -0.5),
        'gla_gate_b': nrm((DEPTH, GLA_HEADS * GLA_KEY), 0.1),
        'gla_norm_g': 1.0 + nrm((DEPTH, GROUP // GLA_HEADS), 0.01),
        'rwkv_mu': jax.random.uniform(next(ks), (DEPTH, sum(RWKV_COLS)), F32),
        'rwkv_w0': decay_base[None, :] + nrm((DEPTH, GROUP), 0.1),
        'rwkv_w2': nrm((DEPTH, RWKV_DECAY_RANK, GROUP), 0.5 * RWKV_DECAY_RANK ** -0.5),
        'rwkv_a0': nrm((DEPTH, GROUP), 0.1),
        'rwkv_a2': nrm((DEPTH, RWKV_A_RANK, GROUP), 0.5 * RWKV_A_RANK ** -0.5),
        'rwkv_g2': nrm((DEPTH, RWKV_GATE_RANK, GROUP), RWKV_GATE_RANK ** -0.5),
        'rwkv_k_k': 0.85 + nrm((DEPTH, GROUP), 0.02),
        'rwkv_k_a': 1.0 + nrm((DEPTH, GROUP), 0.02),
        'rwkv_r_k': -0.04 + nrm((DEPTH, RWKV_HEADS, RWKV_HEAD), 0.02),
        'rwkv_lnx_g': 1.0 + nrm((DEPTH, GROUP), 0.01),
        'rwkv_lnx_b': nrm((DEPTH, GROUP), 0.01),
        'rwkv_v0': 1.0 + nrm((DEPTH - 1, GROUP), 0.1),
        'rwkv_v1': nrm((DEPTH - 1, GROUP, RWKV_V_RANK), GROUP ** -0.5),
        'rwkv_v2': nrm((DEPTH - 1, RWKV_V_RANK, GROUP), 0.5 * RWKV_V_RANK ** -0.5),
        'ffn_w_gate': nrm((N_DENSE, D, FFN_DENSE), D ** -0.5),
        'ffn_w_up': nrm((N_DENSE, D, FFN_DENSE), D ** -0.5),
        'ffn_w_down': nrm((N_DENSE, FFN_DENSE, D), DN_BETA * FFN_DENSE ** -0.5),
        'moe_router': nrm((N_MOE, D, N_EXPERTS), D ** -0.5),
        'moe_w_gate': nrm((N_MOE, N_EXPERTS, D, FFN_EXPERT), D ** -0.5),
        'moe_w_up': nrm((N_MOE, N_EXPERTS, D, FFN_EXPERT), D ** -0.5),
        'moe_w_down': nrm((N_MOE, N_EXPERTS, FFN_EXPERT, D), DN_BETA * FFN_EXPERT ** -0.5),
    }


def reference(x, positions, w_in, w_out, ln1_g, ln1_b, ln2_g, ln2_b, hgrn_lb_logits, hgrn_norm_g,
              gla_gate_w2, gla_gate_b, gla_norm_g, rwkv_mu, rwkv_w0, rwkv_w2, rwkv_a0, rwkv_a2,
              rwkv_g2, rwkv_k_k, rwkv_k_a, rwkv_r_k, rwkv_lnx_g, rwkv_lnx_b, rwkv_v0, rwkv_v1,
              rwkv_v2, ffn_w_gate, ffn_w_up, ffn_w_down, moe_router, moe_w_gate, moe_w_up,
              moe_w_down):
    out_dtype = x.dtype
    h = x.astype(F32)
    lb_all = jnp.cumsum(jax.nn.softmax(hgrn_lb_logits.astype(F32), axis=0), axis=0)
    lb_all = lb_all - lb_all[:1]
    log_gamma = jnp.log(1.0 - 2.0 ** (-5.0 - jnp.arange(RET_HEADS, dtype=F32)))
    v_first = None
    for l in range(DEPTH):
        p = jnp.einsum('btd,dc->btc', h, w_in[l]).astype(F32)
        pa, pb, pc, pd = _split(p, MIXER_COLS)
        ya = _hgrn2_mixer(pa, lb_all[l], hgrn_norm_g[l])
        yb = _gla_mixer(pb, gla_gate_w2[l], gla_gate_b[l], gla_norm_g[l])
        v_mix = None if l == 0 else (rwkv_v0[l - 1], rwkv_v1[l - 1], rwkv_v2[l - 1])
        yc, v_first = _rwkv7_mixer(pc, v_first, v_mix, rwkv_mu[l], rwkv_w0[l], rwkv_w2[l],
                                   rwkv_a0[l], rwkv_a2[l], rwkv_g2[l], rwkv_k_k[l], rwkv_k_a[l],
                                   rwkv_r_k[l], rwkv_lnx_g[l], rwkv_lnx_b[l])
        yd = _retnet_mixer(pd, positions, log_gamma)
        mix = jnp.einsum('btc,cd->btd', jnp.concatenate([ya, yb, yc, yd], axis=-1), w_out[l])
        h = _layer_norm(DN_ALPHA * h + mix, ln1_g[l], ln1_b[l])
        if l % 2 == 0:
            i = l // 2
            f = _swiglu(h, ffn_w_gate[i], ffn_w_up[i], ffn_w_down[i])
        else:
            i = l // 2
            f = _moe(h, moe_router[i], moe_w_gate[i], moe_w_up[i], moe_w_down[i])
        h = _layer_norm(DN_ALPHA * h + f, ln2_g[l], ln2_b[l])
    return h.astype(out_dtype)
```

```python
import functools

import jax
import jax.numpy as jnp
from jax import lax
from jax.experimental import pallas as pl
from jax.experimental.pallas import tpu as pltpu

F32 = jnp.float32
BF16 = jnp.bfloat16

D_MODEL = 1024
DEPTH = 4
GROUP = 256
NH = 4
HD = 64
CH = 64
GLA_KEY = 32
GLA_GATE_NORM = 16.0
RWKV_LNX_EPS = 64e-5
ROPE_BASE = 10000.0
FFN_DENSE = 2816
N_EXPERTS = 8
FFN_EXPERT = 1408
DN_ALPHA = (2.0 * DEPTH) ** 0.25
LN_EPS = 1e-5

HG0 = 0
GL0 = 1024
RW0 = 1920
RWW = 1152
RT0 = 3072
IN_PAD = 4096

TB_MIX = 256
TM_FFN = 1024
FB_FFN = 256
TM_MOE = 512
VMEM_LIMIT = 56 * 1024 * 1024


def _dot(a, b):
    return jnp.dot(a.astype(BF16), b.astype(BF16), preferred_element_type=F32)


def _dot_nt(a, b):
    return lax.dot_general(a.astype(BF16), b.astype(BF16), (((1,), (1,)), ((), ())),
                           preferred_element_type=F32)


def _dot_tn(a, b):
    return lax.dot_general(a.astype(BF16), b.astype(BF16), (((0,), (0,)), ((), ())),
                           preferred_element_type=F32)


def _split2(x):
    hi = x.astype(BF16)
    lo = (x - hi.astype(F32)).astype(BF16)
    return hi, lo


def _split3(x):
    hi = x.astype(BF16)
    r1 = x - hi.astype(F32)
    mid = r1.astype(BF16)
    lo = (r1 - mid.astype(F32)).astype(BF16)
    return hi, mid, lo


def _dot_hl(a, w_ref):
    a_hi, a_lo = _split2(a)
    w_hi = w_ref[0]
    w_lo = w_ref[1]
    return (jnp.dot(a_hi, w_hi, preferred_element_type=F32)
            + jnp.dot(a_lo, w_hi, preferred_element_type=F32)
            + jnp.dot(a_hi, w_lo, preferred_element_type=F32))


def _sigmoid(x):
    return 1.0 / (1.0 + jnp.exp(-x))


def _silu(x):
    return x * _sigmoid(x)


def _softplus(x):
    return jnp.maximum(x, 0.0) + jnp.log(1.0 + jnp.exp(-jnp.abs(x)))


def _layer_norm(x, g, b):
    mu = jnp.mean(x, axis=-1, keepdims=True)
    xc = x - mu
    var = jnp.mean(xc * xc, axis=-1, keepdims=True)
    return xc * lax.rsqrt(var + LN_EPS) * g + b


def _cumsum_rows(x, tri):
    hi, mid, lo = _split3(x)
    return (jnp.dot(tri, hi, preferred_element_type=F32)
            + jnp.dot(tri, mid, preferred_element_type=F32)
            + jnp.dot(tri, lo, preferred_element_type=F32))


def _head_mean(x, hs):
    hi, lo = _split2(x)
    s = jnp.dot(hi, hs, preferred_element_type=F32) + jnp.dot(lo, hs, preferred_element_type=F32)
    return s * (1.0 / HD)


def _stack_heads(x, group):
    lane = lax.broadcasted_iota(jnp.int32, (1, x.shape[1]), 1)
    parts = [jnp.where((lane // group) == h, x, 0.0) for h in range(NH)]
    return jnp.concatenate(parts, axis=0)


def _gla_chunk(q, k, v, logf, st_ref, kgroup, tri):
    c = q.shape[0]
    kt = q.shape[1]
    cum = _cumsum_rows(logf, tri)
    mid = cum[c // 2:c // 2 + 1]
    cend = cum[c - 1:c]
    qi = q * jnp.exp(cum)
    qa = q * jnp.exp(cum - mid)
    ka = k * jnp.exp(mid - cum)
    ks = k * jnp.exp(cend - cum)
    a = _dot_nt(_stack_heads(qa, kgroup), ka)
    row = lax.broadcasted_iota(jnp.int32, a.shape, 0) & (c - 1)
    col = lax.broadcasted_iota(jnp.int32, a.shape, 1)
    a = jnp.where(row >= col, a, 0.0).astype(BF16)
    st = st_ref[...]
    o = _dot_nt(qi, st)
    vb = v.astype(BF16)
    lane_v = lax.broadcasted_iota(jnp.int32, (1, NH * HD), 1)
    for h in range(NH):
        oh = jnp.dot(a[h * c:(h + 1) * c], vb, preferred_element_type=F32)
        o = o + jnp.where((lane_v // HD) == h, oh, 0.0)
    upd = _dot_tn(v, ks)
    r_h = lax.broadcasted_iota(jnp.int32, (NH * HD, kt), 0) // HD
    c_h = lax.broadcasted_iota(jnp.int32, (NH * HD, kt), 1) // kgroup
    st_ref[...] = st * jnp.exp(cend) + jnp.where(r_h == c_h, upd, 0.0)
    return o


def _rwkv_chunk(r, logw, k, v, kk, a, st_ref, tri):
    c = r.shape[0]
    n = NH * c
    cum = _cumsum_rows(logw, tri)
    cump = cum - logw
    mid = cum[c // 2:c // 2 + 1]
    cend = cum[c - 1:c]
    eq = jnp.exp(cum - mid)
    eqp = jnp.exp(cump - mid)
    ek = jnp.exp(mid - cum)
    es = jnp.exp(cend - cum)
    ei = jnp.exp(cum)
    eip = jnp.exp(cump)
    b = kk * a
    na = -kk

    def stack(x):
        return _stack_heads(x, HD).astype(BF16)

    lhs = jnp.concatenate([stack(na * eqp), stack(r * eq)], axis=0)
    rhs = jnp.concatenate([stack(b * ek), stack(k * ek)], axis=0)
    sc = lax.dot_general(lhs, rhs, (((1,), (1,)), ((), ())), preferred_element_type=F32)
    i0 = lax.broadcasted_iota(jnp.int32, (n, n), 0)
    i1 = lax.broadcasted_iota(jnp.int32, (n, n), 1)
    strict = (i0 & (c - 1)) > (i1 & (c - 1))
    incl = (i0 & (c - 1)) >= (i1 & (c - 1))
    a_ab = jnp.where(strict, sc[:n, :n], 0.0)
    a_ak = jnp.where(strict, sc[:n, n:], 0.0).astype(BF16)
    a_rb = jnp.where(incl, sc[n:, :n], 0.0).astype(BF16)
    a_rk = jnp.where(incl, sc[n:, n:], 0.0).astype(BF16)

    same16 = (i0 >> 4) == (i1 >> 4)
    eye = jnp.where(i0 == i1, 1.0, 0.0).astype(F32)
    dm = jnp.where(same16, a_ab, 0.0)
    em = a_ab - dm
    p = eye + dm
    d2 = _dot(dm, dm)
    p = p + _dot(p, d2)
    d4 = _dot(d2, d2)
    p = p + _dot(p, d4)
    d8 = _dot(d4, d4)
    p = p + _dot(p, d8)
    f = _dot(p, em)
    f2 = _dot(f, f)
    g = eye + f + f2 + _dot(f, f2)
    t = _dot(g, p)

    st = st_ref[...]
    vs = stack(v)
    rhs_u = _dot_nt(stack(na * eip), st) + jnp.dot(a_ak, vs, preferred_element_type=F32)
    us = _dot(t, rhs_u)
    usb = us.astype(BF16)
    os_ = (_dot_nt(stack(r * ei), st) + jnp.dot(a_rb, usb, preferred_element_type=F32)
           + jnp.dot(a_rk, vs, preferred_element_type=F32))
    o = os_[0:c] + os_[c:2 * c] + os_[2 * c:3 * c] + os_[3 * c:4 * c]
    st_ref[...] = (st * jnp.exp(cend) + _dot_tn(usb, stack(b * es)) + _dot_tn(vs, stack(k * es)))
    return o


def _mixer_kernel(first, tb, *refs):
    if first:
        (h_ref, cos_ref, sin_ref, win_ref, wout_ref, vec_ref, mu_ref, ln_ref, glaw2_ref, w2a2_ref,
         g2_ref, hs_ref, out_ref, vf_out_ref, p_scr, y_scr, st_hg, st_gl, st_rw, st_rt) = refs
        vf_in_ref = v1_ref = v2_ref = None
    else:
        (h_ref, cos_ref, sin_ref, win_ref, wout_ref, vec_ref, mu_ref, ln_ref, glaw2_ref, w2a2_ref,
         g2_ref, hs_ref, vf_in_ref, v1_ref, v2_ref, out_ref, p_scr, y_scr, st_hg, st_gl, st_rw,
         st_rt) = refs
        vf_out_ref = None
    c = CH
    t_idx = pl.program_id(1)

    @pl.when(t_idx == 0)
    def _():
        st_hg[...] = jnp.zeros_like(st_hg)
        st_gl[...] = jnp.zeros_like(st_gl)
        st_rw[...] = jnp.zeros_like(st_rw)
        st_rt[...] = jnp.zeros_like(st_rt)
        p_scr[0:8, RW0:RW0 + RWW] = jnp.zeros((8, RWW), F32)

    @pl.when(t_idx > 0)
    def _():
        p_scr[0:8, RW0:RW0 + RWW] = p_scr[tb:tb + 8, RW0:RW0 + RWW]

    hb = h_ref[...].astype(BF16)
    for j in range(IN_PAD // 1024):
        p_scr[8:tb + 8, j * 1024:(j + 1) * 1024] = jnp.dot(
            hb, win_ref[:, j * 1024:(j + 1) * 1024], preferred_element_type=F32)

    def vec(i, n=GROUP):
        return vec_ref[i:i + 1, 0:n]

    def chunk_body(ci, carry):
        r0 = pl.multiple_of(ci * c, c)
        rows = pl.ds(r0 + 8, c)
        yrows = pl.ds(r0, c)
        ti = lax.broadcasted_iota(jnp.int32, (c, c), 0)
        tj = lax.broadcasted_iota(jnp.int32, (c, c), 1)
        tri = jnp.where(ti >= tj, 1.0, 0.0).astype(BF16)
        hs = hs_ref[...]

        q_raw = p_scr[rows, HG0:HG0 + 256]
        f_raw = p_scr[rows, HG0 + 256:HG0 + 512]
        i_raw = p_scr[rows, HG0 + 512:HG0 + 768]
        g_raw = p_scr[rows, HG0 + 768:HG0 + 1024]
        lb = vec(0)
        fg = lb + (1.0 - lb) * _sigmoid(f_raw)
        o = _gla_chunk(_silu(q_raw), 1.0 - fg, i_raw, jnp.log(fg), st_hg, HD, tri)
        o = o * lax.rsqrt(_head_mean(o * o, hs) + 1e-6) * vec(1) * _silu(g_raw)
        y_scr[yrows, 0:256] = o.astype(BF16)

        q = p_scr[rows, GL0:GL0 + 128] * (GLA_KEY ** -0.5)
        k = p_scr[rows, GL0 + 128:GL0 + 256]
        v = p_scr[rows, GL0 + 256:GL0 + 512]
        rg = p_scr[rows, GL0 + 512:GL0 + 768]
        glr = p_scr[rows, GL0 + 768:GL0 + 896]
        z = _dot_hl(glr, glaw2_ref) + vec(2, 128)
        logf = -_softplus(-z) * (1.0 / GLA_GATE_NORM)
        o = _gla_chunk(q, k, v, logf, st_gl, GLA_KEY, tri)
        o = o * lax.rsqrt(_head_mean(o * o, hs) + 1e-6) * vec(3) * _silu(rg)
        y_scr[yrows, 256:512] = o.astype(BF16)

        xs = p_scr[pl.ds(r0, c + 8), RW0:RW0 + RWW]
        cur = xs[8:]
        prev = pltpu.roll(xs, 1, 0)[8:]
        pm = cur + (prev - cur) * mu_ref[0:1, :]
        r = pm[:, 0:256]
        k = pm[:, 256:512]
        v = pm[:, 512:768]
        wa = pm[:, 768:896]
        lane = lax.broadcasted_iota(jnp.int32, (1, 128), 1)
        pre = _dot_hl(jnp.where(lane < 64, jnp.tanh(wa), wa), w2a2_ref)
        w = -_softplus(-(vec(4) + pre[:, 0:256])) - 0.5
        logw = -jnp.exp(w)
        a = _sigmoid(vec(5) + pre[:, 256:512])
        gate = _dot(_sigmoid(pm[:, 896:1152]), g2_ref[...])
        if first:
            vf_out_ref[yrows, :] = v
        else:
            vf = vf_in_ref[yrows, :]
            v = v + (vf - v) * _sigmoid(vec(11) + _dot(_dot(v, v1_ref[...]), v2_ref[...]))
        kk = k * vec(6)
        kk = kk / jnp.maximum(jnp.sqrt(_head_mean(kk * kk, hs) * HD), 1e-12)
        k = k * (1.0 + (a - 1.0) * vec(7))
        o = _rwkv_chunk(r, logw, k, v, kk, a, st_rw, tri)
        mu_o = _head_mean(o, hs)
        oc = o - mu_o
        o = oc * lax.rsqrt(_head_mean(oc * oc, hs) + RWKV_LNX_EPS) * vec(9) + vec(10)
        o = o + (_head_mean(r * k * vec(8), hs) * HD) * v
        y_scr[yrows, 512:768] = (o * gate).astype(BF16)

        q = p_scr[rows, RT0:RT0 + 256]
        k = p_scr[rows, RT0 + 256:RT0 + 512]
        v = p_scr[rows, RT0 + 512:RT0 + 768]
        gr = p_scr[rows, RT0 + 768:RT0 + 1024]
        cs = cos_ref[yrows, :]
        sn = sin_ref[yrows, :]
        cs = jnp.concatenate([cs, cs], axis=1)
        sn = jnp.concatenate([sn, sn], axis=1)
        lane_r = lax.broadcasted_iota(jnp.int32, (1, 256), 1)
        first_half = (lane_r & (HD - 1)) < (HD // 2)

        def rot(x):
            swapped = jnp.where(first_half, pltpu.roll(x, 256 - HD // 2, 1), pltpu.roll(x, HD // 2, 1))
            return x * cs + swapped * sn

        logg = jnp.broadcast_to(vec(12), (c, 256))
        o = _gla_chunk(rot(q), rot(k) * (HD ** -0.5), v, logg, st_rt, HD, tri)
        mu_o = _head_mean(o, hs)
        oc = o - mu_o
        o = oc * lax.rsqrt(_head_mean(oc * oc, hs) + LN_EPS)
        y_scr[yrows, 768:1024] = (_silu(gr) * o).astype(BF16)
        return carry

    lax.fori_loop(0, tb // c, chunk_body, 0)

    mix = jnp.dot(y_scr[...], wout_ref[...], preferred_element_type=F32)
    out_ref[...] = _layer_norm(DN_ALPHA * h_ref[...] + mix, ln_ref[0:1, :], ln_ref[1:2, :])


def _mixer_call(first, h, cos_t, sin_t, w_in, w_out, vecs, mu, ln, glaw2, w2a2, g2, hs,
                vf=None, v1=None, v2=None):
    b, s, d = h.shape
    tb = TB_MIX
    grid = (b, s // tb)

    def tok(width):
        return pl.BlockSpec((None, tb, width), lambda i, j: (i, j, 0))

    def full(arr):
        nd = arr.ndim
        return pl.BlockSpec(arr.shape, lambda i, j: (0,) * nd)

    args = [h, cos_t, sin_t, w_in, w_out, vecs, mu, ln, glaw2, w2a2, g2, hs]
    in_specs = [tok(d), tok(128), tok(128)] + [full(a) for a in args[3:]]
    out_h = jax.ShapeDtypeStruct((b, s, d), F32)
    if first:
        out_shape = (out_h, jax.ShapeDtypeStruct((b, s, GROUP), F32))
        out_specs = (tok(d), tok(GROUP))
    else:
        args += [vf, v1, v2]
        in_specs += [tok(GROUP), full(v1), full(v2)]
        out_shape = out_h
        out_specs = tok(d)
    scratch = [
        pltpu.VMEM((tb + 8, IN_PAD), F32),
        pltpu.VMEM((tb, d), BF16),
        pltpu.VMEM((GROUP, GROUP), F32),
        pltpu.VMEM((GROUP, NH * GLA_KEY), F32),
        pltpu.VMEM((GROUP, GROUP), F32),
        pltpu.VMEM((GROUP, GROUP), F32),
    ]
    return pl.pallas_call(
        functools.partial(_mixer_kernel, first, tb),
        grid=grid,
        in_specs=in_specs,
        out_specs=out_specs,
        out_shape=out_shape,
        scratch_shapes=scratch,
        compiler_params=pltpu.CompilerParams(
            dimension_semantics=("arbitrary", "arbitrary"), vmem_limit_bytes=VMEM_LIMIT),
        name="token_mixer_first" if first else "token_mixer",
    )(*args)


def _ffn_kernel(h_ref, wg_ref, wu_ref, wd_ref, ln_ref, out_ref, hb_scr, acc_scr):
    j = pl.program_id(1)

    @pl.when(j == 0)
    def _():
        hb_scr[...] = h_ref[...].astype(BF16)
        acc_scr[...] = jnp.zeros_like(acc_scr)

    hb = hb_scr[...]
    g = jnp.dot(hb, wg_ref[...], preferred_element_type=F32)
    u = jnp.dot(hb, wu_ref[...], preferred_element_type=F32)
    act = (_silu(g) * u).astype(BF16)
    acc_scr[...] += jnp.dot(act, wd_ref[...], preferred_element_type=F32)

    @pl.when(j == pl.num_programs(1) - 1)
    def _():
        out_ref[...] = _layer_norm(DN_ALPHA * h_ref[...] + acc_scr[...], ln_ref[0:1, :], ln_ref[1:2, :])


def _ffn_call(h2, wg, wu, wd, ln):
    t, d = h2.shape
    f = wg.shape[1]
    tm, fb = TM_FFN, FB_FFN
    return pl.pallas_call(
        _ffn_kernel,
        grid=(t // tm, f // fb),
        in_specs=[
            pl.BlockSpec((tm, d), lambda i, j: (i, 0)),
            pl.BlockSpec((d, fb), lambda i, j: (0, j)),
            pl.BlockSpec((d, fb), lambda i, j: (0, j)),
            pl.BlockSpec((fb, d), lambda i, j: (j, 0)),
            pl.BlockSpec(ln.shape, lambda i, j: (0, 0)),
        ],
        out_specs=pl.BlockSpec((tm, d), lambda i, j: (i, 0)),
        out_shape=jax.ShapeDtypeStruct((t, d), F32),
        scratch_shapes=[pltpu.VMEM((tm, d), BF16), pltpu.VMEM((tm, d), F32)],
        compiler_params=pltpu.CompilerParams(
            dimension_semantics=("arbitrary", "arbitrary"), vmem_limit_bytes=VMEM_LIMIT),
        name="ffn_dense",
    )(h2, wg, wu, wd, ln)


def _moe_kernel(h_ref, rt_ref, wg_ref, wu_ref, wd_ref, ln_ref, out_ref, hb_scr, gate_scr, acc_scr):
    e = pl.program_id(1)

    @pl.when(e == 0)
    def _():
        h = h_ref[...]
        hb_scr[...] = h.astype(BF16)
        acc_scr[...] = jnp.zeros_like(acc_scr)
        logits = _dot_hl(h, rt_ref)
        lane = lax.broadcasted_iota(jnp.int32, logits.shape, 1)
        neg = jnp.float32(-jnp.inf)
        logits = jnp.where(lane < N_EXPERTS, logits, neg)
        m1 = jnp.max(logits, axis=-1, keepdims=True)
        i1 = jnp.min(jnp.where(logits == m1, lane, 128), axis=-1, keepdims=True)
        rest = jnp.where(lane == i1, neg, logits)
        m2 = jnp.max(rest, axis=-1, keepdims=True)
        i2 = jnp.min(jnp.where(rest == m2, lane, 128), axis=-1, keepdims=True)
        e2 = jnp.exp(m2 - m1)
        g1 = 1.0 / (1.0 + e2)
        g2 = e2 / (1.0 + e2)
        gate_scr[...] = jnp.where(lane == i1, g1, 0.0) + jnp.where(lane == i2, g2, 0.0)

    hb = hb_scr[...]
    g = jnp.dot(hb, wg_ref[...], preferred_element_type=F32)
    u = jnp.dot(hb, wu_ref[...], preferred_element_type=F32)
    act = (_silu(g) * u).astype(BF16)
    y = jnp.dot(act, wd_ref[...], preferred_element_type=F32)
    gates = gate_scr[...]
    lane = lax.broadcasted_iota(jnp.int32, gates.shape, 1)
    ge = jnp.sum(jnp.where(lane == e, gates, 0.0), axis=-1, keepdims=True)
    acc_scr[...] += ge * y

    @pl.when(e == pl.num_programs(1) - 1)
    def _():
        out_ref[...] = _layer_norm(DN_ALPHA * h_ref[...] + acc_scr[...], ln_ref[0:1, :], ln_ref[1:2, :])


def _moe_call(h2, router_hl, wg, wu, wd, ln):
    t, d = h2.shape
    ne, _, f = wg.shape
    tm = TM_MOE
    return pl.pallas_call(
        _moe_kernel,
        grid=(t // tm, ne),
        in_specs=[
            pl.BlockSpec((tm, d), lambda i, e: (i, 0)),
            pl.BlockSpec(router_hl.shape, lambda i, e: (0, 0, 0)),
            pl.BlockSpec((None, d, f), lambda i, e: (e, 0, 0)),
            pl.BlockSpec((None, d, f), lambda i, e: (e, 0, 0)),
            pl.BlockSpec((None, f, d), lambda i, e: (e, 0, 0)),
            pl.BlockSpec(ln.shape, lambda i, e: (0, 0)),
        ],
        out_specs=pl.BlockSpec((tm, d), lambda i, e: (i, 0)),
        out_shape=jax.ShapeDtypeStruct((t, d), F32),
        scratch_shapes=[pltpu.VMEM((tm, d), BF16), pltpu.VMEM((tm, 128), F32), pltpu.VMEM((tm, d), F32)],
        compiler_params=pltpu.CompilerParams(
            dimension_semantics=("arbitrary", "arbitrary"), vmem_limit_bytes=VMEM_LIMIT),
        name="ffn_experts",
    )(h2, router_hl, wg, wu, wd, ln)


def _trig_kernel(ang_ref, cos_ref, sin_ref):
    ang = ang_ref[...]
    cos_ref[...] = jnp.cos(ang)
    sin_ref[...] = jnp.sin(ang)


def _rotary_tables(positions):
    b, s = positions.shape
    half = HD // 2
    inv_freq = ROPE_BASE ** (-jnp.arange(half, dtype=F32) / half)
    ang = positions.astype(F32)[:, :, None] * inv_freq
    rows = b * s * half // 128
    ang2 = ang.reshape(rows, 128)
    blk = min(rows, 1024)
    cos2, sin2 = pl.pallas_call(
        _trig_kernel,
        grid=(rows // blk,),
        in_specs=[pl.BlockSpec((blk, 128), lambda i: (i, 0))],
        out_specs=(pl.BlockSpec((blk, 128), lambda i: (i, 0)),) * 2,
        out_shape=(jax.ShapeDtypeStruct((rows, 128), F32),) * 2,
        name="rotary_tables",
    )(ang2)
    cos = cos2.reshape(b, s, half)
    sin = sin2.reshape(b, s, half)
    cos_t = jnp.concatenate([cos, cos, cos, cos], axis=-1)
    sin_t = jnp.concatenate([-sin, sin, -sin, sin], axis=-1)
    return cos_t, sin_t


def _hl(w):
    hi = w.astype(BF16)
    lo = (w - hi.astype(F32)).astype(BF16)
    return jnp.stack([hi, lo])


def _pad_to(a, shape):
    return jnp.pad(a, [(0, t - s) for s, t in zip(a.shape, shape)])


def _pack_w_in(w):
    d = w.shape[0]
    z = lambda n: jnp.zeros((d, n), w.dtype)
    gl, rw, rt = 1024, 1808, 2864
    cols = [
        w[:, 0:1024],
        w[:, gl:gl + 512], w[:, gl + 528:gl + 784], w[:, gl + 512:gl + 528], z(112),
        w[:, rw:rw + 1056], z(96),
        w[:, rt:rt + 1024],
    ]
    return jnp.concatenate(cols, axis=1).astype(BF16)


def kernel(x, positions, w_in, w_out, ln1_g, ln1_b, ln2_g, ln2_b, hgrn_lb_logits, hgrn_norm_g,
           gla_gate_w2, gla_gate_b, gla_norm_g, rwkv_mu, rwkv_w0, rwkv_w2, rwkv_a0, rwkv_a2,
           rwkv_g2, rwkv_k_k, rwkv_k_a, rwkv_r_k, rwkv_lnx_g, rwkv_lnx_b, rwkv_v0, rwkv_v1,
           rwkv_v2, ffn_w_gate, ffn_w_up, ffn_w_down, moe_router, moe_w_gate, moe_w_up,
           moe_w_down):
    out_dtype = x.dtype
    b, s, d = x.shape
    h = x.astype(F32)
    cos_t, sin_t = _rotary_tables(positions)

    lb_all = jnp.cumsum(jax.nn.softmax(hgrn_lb_logits.astype(F32), axis=0), axis=0)
    lb_all = lb_all - lb_all[:1]
    log_gamma = jnp.log(1.0 - 2.0 ** (-5.0 - jnp.arange(NH, dtype=F32)))
    log_gamma_row = jnp.repeat(log_gamma, HD)
    head_i = jnp.arange(GROUP) // HD
    hs = (head_i[:, None] == head_i[None, :]).astype(BF16)
    tile4 = lambda v: jnp.tile(v, NH)
    zrow = jnp.zeros((GROUP,), F32)

    v_first = None
    for l in range(DEPTH):
        rows = [
            lb_all[l], tile4(hgrn_norm_g[l]), _pad_to(gla_gate_b[l], (GROUP,)), tile4(gla_norm_g[l]),
            rwkv_w0[l], rwkv_a0[l], rwkv_k_k[l], rwkv_k_a[l], rwkv_r_k[l].reshape(GROUP),
            rwkv_lnx_g[l], rwkv_lnx_b[l], rwkv_v0[l - 1] if l > 0 else zrow, log_gamma_row,
            zrow, zrow, zrow,
        ]
        vecs = jnp.stack(rows).astype(F32)
        mu = _pad_to(rwkv_mu[l][None, :], (8, RWW))
        ln1 = _pad_to(jnp.stack([ln1_g[l], ln1_b[l]]), (8, d))
        ln2 = _pad_to(jnp.stack([ln2_g[l], ln2_b[l]]), (8, d))
        glaw2 = _hl(_pad_to(gla_gate_w2[l], (128, 128)))
        w2a2 = jnp.zeros((128, 512), F32)
        w2a2 = w2a2.at[0:64, 0:256].set(rwkv_w2[l]).at[64:128, 256:512].set(rwkv_a2[l])
        w2a2 = _hl(w2a2)
        g2 = _pad_to(rwkv_g2[l], (256, 256)).astype(BF16)
        win = _pack_w_in(w_in[l])
        wout = w_out[l].astype(BF16)
        if l == 0:
            h, v_first = _mixer_call(True, h, cos_t, sin_t, win, wout, vecs, mu, ln1, glaw2, w2a2,
                                     g2, hs)
        else:
            v1 = _pad_to(rwkv_v1[l - 1], (256, 128)).astype(BF16)
            v2 = _pad_to(rwkv_v2[l - 1], (128, 256)).astype(BF16)
            h = _mixer_call(False, h, cos_t, sin_t, win, wout, vecs, mu, ln1, glaw2, w2a2, g2, hs,
                            v_first, v1, v2)
        h2 = h.reshape(b * s, d)
        i = l // 2
        if l % 2 == 0:
            h2 = _ffn_call(h2, ffn_w_gate[i].astype(BF16), ffn_w_up[i].astype(BF16),
                           ffn_w_down[i].astype(BF16), ln2)
        else:
            router = _hl(_pad_to(moe_router[i], (d, 128)))
            h2 = _moe_call(h2, router, moe_w_gate[i].astype(BF16), moe_w_up[i].astype(BF16),
                           moe_w_down[i].astype(BF16), ln2)
        h = h2.reshape(b, s, d)
    return h.astype(out_dtype)
```

```python
import functools

import jax
import jax.numpy as jnp
from jax import lax
from jax.experimental import pallas as pl
from jax.experimental.pallas import tpu as pltpu

F32 = jnp.float32
BF16 = jnp.bfloat16

D_MODEL = 1024
DEPTH = 4
GROUP = 256
NH = 4
HD = 64
CH = 64
GLA_KEY = 32
GLA_GATE_NORM = 16.0
RWKV_LNX_EPS = 64e-5
ROPE_BASE = 10000.0
FFN_DENSE = 2816
N_EXPERTS = 8
FFN_EXPERT = 1408
DN_ALPHA = (2.0 * DEPTH) ** 0.25
LN_EPS = 1e-5

HG0 = 0
GL0 = 1024
RW0 = 1920
RWW = 1152
RT0 = 3072
IN_PAD = 4096

TB_MIX = 128
TM_FFN = 1024
FB_FFN = 256
TM_MOE = 1024
NSUB_MOE = 2
CSUB_MOE = 160
VMEM_LIMIT = 56 * 1024 * 1024


def _dot(a, b):
    return jnp.dot(a.astype(BF16), b.astype(BF16), preferred_element_type=F32)


def _dot_nt(a, b):
    return lax.dot_general(a.astype(BF16), b.astype(BF16), (((1,), (1,)), ((), ())),
                           preferred_element_type=F32)


def _dot_tn(a, b):
    return lax.dot_general(a.astype(BF16), b.astype(BF16), (((0,), (0,)), ((), ())),
                           preferred_element_type=F32)


def _split2(x):
    hi = x.astype(BF16)
    lo = (x - hi.astype(F32)).astype(BF16)
    return hi, lo


def _dot_hl(a, w_ref):
    a_hi, a_lo = _split2(a)
    w_hi = w_ref[0]
    w_lo = w_ref[1]
    return (jnp.dot(a_hi, w_hi, preferred_element_type=F32)
            + jnp.dot(a_lo, w_hi, preferred_element_type=F32)
            + jnp.dot(a_hi, w_lo, preferred_element_type=F32))


def _sigmoid(x):
    return 1.0 / (1.0 + jnp.exp(-x))


def _silu(x):
    return x * _sigmoid(x)


def _softplus(x):
    return jnp.maximum(x, 0.0) + jnp.log(1.0 + jnp.exp(-jnp.abs(x)))


def _layer_norm(x, g, b):
    mu = jnp.mean(x, axis=-1, keepdims=True)
    xc = x - mu
    var = jnp.mean(xc * xc, axis=-1, keepdims=True)
    return xc * lax.rsqrt(var + LN_EPS) * g + b


def _cumsum_rows(x, tri):
    hi, lo = _split2(x)
    return jnp.dot(tri, hi, preferred_element_type=F32) + jnp.dot(tri, lo, preferred_element_type=F32)


def _head_mean(x, hs):
    hi, lo = _split2(x)
    s = jnp.dot(hi, hs, preferred_element_type=F32) + jnp.dot(lo, hs, preferred_element_type=F32)
    return s * (1.0 / HD)


def _stack_heads(x, group):
    lane = lax.broadcasted_iota(jnp.int32, (1, x.shape[1]), 1)
    parts = [jnp.where((lane // group) == h, x, 0.0) for h in range(NH)]
    return jnp.concatenate(parts, axis=0)


def _run_interleaved(gens):
    alive = list(gens)
    while alive:
        still = []
        for g in alive:
            try:
                next(g)
                still.append(g)
            except StopIteration:
                pass
        alive = still


def _gla_core(q, k, v, st_ref, kgroup, tri, logf=None, factors=None):
    c = q.shape[0]
    kt = q.shape[1]
    if factors is None:
        cum = _cumsum_rows(logf, tri)
        yield
        mid = cum[c // 2:c // 2 + 1]
        cend = cum[c - 1:c]
        e_i = jnp.exp(cum)
        e_a = jnp.exp(cum - mid)
        e_k = jnp.exp(mid - cum)
        e_s = jnp.exp(cend - cum)
        e_end = jnp.exp(cend)
    else:
        e_i, e_a, e_k, e_s, e_end = factors
    a = _dot_nt(_stack_heads(q * e_a, kgroup), k * e_k)
    st = st_ref[...]
    o_int = _dot_nt(q * e_i, st)
    upd = _dot_tn(v, k * e_s)
    yield
    row = lax.broadcasted_iota(jnp.int32, a.shape, 0) & (c - 1)
    col = lax.broadcasted_iota(jnp.int32, a.shape, 1)
    a = jnp.where(row >= col, a, 0.0).astype(BF16)
    r_h = lax.broadcasted_iota(jnp.int32, (NH * HD, kt), 0) // HD
    c_h = lax.broadcasted_iota(jnp.int32, (NH * HD, kt), 1) // kgroup
    st_ref[...] = st * e_end + jnp.where(r_h == c_h, upd, 0.0)
    vb = v.astype(BF16)
    ohs = [jnp.dot(a[h * c:(h + 1) * c], vb, preferred_element_type=F32) for h in range(NH)]
    yield
    lane_v = lax.broadcasted_iota(jnp.int32, (1, NH * HD), 1)
    o = o_int
    for h in range(NH):
        o = o + jnp.where((lane_v // HD) == h, ohs[h], 0.0)
    return o


def _rwkv_core(r, logw, k, v, kk, a, st_ref, tri):
    c = r.shape[0]
    n = NH * c
    cum = _cumsum_rows(logw, tri)
    yield
    cump = cum - logw
    mid = cum[c // 2:c // 2 + 1]
    cend = cum[c - 1:c]
    eq = jnp.exp(cum - mid)
    eqp = jnp.exp(cump - mid)
    ek = jnp.exp(mid - cum)
    es = jnp.exp(cend - cum)
    ei = jnp.exp(cum)
    eip = jnp.exp(cump)
    b = kk * a
    na = -kk

    def stack(x):
        return _stack_heads(x, HD).astype(BF16)

    lhs = jnp.concatenate([stack(na * eqp), stack(r * eq)], axis=0)
    rhs = jnp.concatenate([stack(b * ek), stack(k * ek)], axis=0)
    sc = lax.dot_general(lhs, rhs, (((1,), (1,)), ((), ())), preferred_element_type=F32)
    st = st_ref[...]
    stb = st.astype(BF16)
    rs_u = lax.dot_general(stack(na * eip), stb, (((1,), (1,)), ((), ())), preferred_element_type=F32)
    rs_o = lax.dot_general(stack(r * ei), stb, (((1,), (1,)), ((), ())), preferred_element_type=F32)
    vs = stack(v)
    s_v = _dot_tn(vs, stack(k * es))
    bs = stack(b * es)
    yield
    i0 = lax.broadcasted_iota(jnp.int32, (n, n), 0)
    i1 = lax.broadcasted_iota(jnp.int32, (n, n), 1)
    strict = (i0 & (c - 1)) > (i1 & (c - 1))
    incl = (i0 & (c - 1)) >= (i1 & (c - 1))
    a_ab = jnp.where(strict, sc[:n, :n], 0.0)
    a_ak = jnp.where(strict, sc[:n, n:], 0.0).astype(BF16)
    a_rb = jnp.where(incl, sc[n:, :n], 0.0).astype(BF16)
    a_rk = jnp.where(incl, sc[n:, n:], 0.0).astype(BF16)
    same16 = (i0 >> 4) == (i1 >> 4)
    eye = jnp.where(i0 == i1, 1.0, 0.0).astype(F32)
    dm = jnp.where(same16, a_ab, 0.0)
    em = (a_ab - dm).astype(BF16)
    p = eye + dm
    dmb = dm.astype(BF16)
    d2 = jnp.dot(dmb, dmb, preferred_element_type=F32)
    ak_v = jnp.dot(a_ak, vs, preferred_element_type=F32)
    rk_v = jnp.dot(a_rk, vs, preferred_element_type=F32)
    yield
    d2b = d2.astype(BF16)
    pd = _dot(p, d2b)
    d4 = jnp.dot(d2b, d2b, preferred_element_type=F32)
    yield
    p = p + pd
    d4b = d4.astype(BF16)
    pd = _dot(p, d4b)
    d8 = jnp.dot(d4b, d4b, preferred_element_type=F32)
    yield
    p = p + pd
    pd = _dot(p, d8)
    yield
    p = p + pd
    pb = p.astype(BF16)
    f = jnp.dot(pb, em, preferred_element_type=F32)
    yield
    fb = f.astype(BF16)
    f2 = jnp.dot(fb, fb, preferred_element_type=F32)
    yield
    ff2 = _dot(fb, f2)
    yield
    g = eye + f + f2 + ff2
    t = _dot(g, pb)
    yield
    us = _dot(t, rs_u + ak_v)
    yield
    usb = us.astype(BF16)
    rb_u = jnp.dot(a_rb, usb, preferred_element_type=F32)
    s_u = _dot_tn(usb, bs)
    yield
    os_ = rs_o + rb_u + rk_v
    o = os_[0:c] + os_[c:2 * c] + os_[2 * c:3 * c] + os_[3 * c:4 * c]
    st_ref[...] = st * jnp.exp(cend) + s_u + s_v
    return o


def _mixer_kernel(first, nb, tb, *refs):
    if first:
        (h_ref, cos_ref, sin_ref, win_ref, wout_ref, vec_ref, mu_ref, ln_ref, glaw2_ref, w2a2_ref,
         g2_ref, hs_ref, rfac_ref, out_ref, vf_out_ref, p_scr, y_scr, st_hg, st_gl, st_rw,
         st_rt) = refs
        vf_in_ref = v1_ref = v2_ref = None
    else:
        (h_ref, cos_ref, sin_ref, win_ref, wout_ref, vec_ref, mu_ref, ln_ref, glaw2_ref, w2a2_ref,
         g2_ref, hs_ref, rfac_ref, vf_in_ref, v1_ref, v2_ref, out_ref, p_scr, y_scr, st_hg, st_gl,
         st_rw, st_rt) = refs
        vf_out_ref = None
    c = CH
    t_idx = pl.program_id(0)

    @pl.when(t_idx == 0)
    def _():
        st_hg[...] = jnp.zeros_like(st_hg)
        st_gl[...] = jnp.zeros_like(st_gl)
        st_rw[...] = jnp.zeros_like(st_rw)
        st_rt[...] = jnp.zeros_like(st_rt)
        p_scr[:, 0:8, RW0:RW0 + RWW] = jnp.zeros((nb, 8, RWW), F32)

    @pl.when(t_idx > 0)
    def _():
        p_scr[:, 0:8, RW0:RW0 + RWW] = p_scr[:, tb:tb + 8, RW0:RW0 + RWW]

    hb = h_ref[...].reshape(nb * tb, D_MODEL).astype(BF16)
    for j in range(IN_PAD // 1024):
        res = jnp.dot(hb, win_ref[:, j * 1024:(j + 1) * 1024], preferred_element_type=F32)
        for bi in range(nb):
            p_scr[bi, 8:tb + 8, j * 1024:(j + 1) * 1024] = res[bi * tb:(bi + 1) * tb]

    def vec(i, n=GROUP):
        return vec_ref[i:i + 1, 0:n]

    def chunk_body(ci, carry):
        r0 = pl.multiple_of(ci * c, c)
        rows = pl.ds(r0 + 8, c)
        crow = pl.ds(r0, c)
        ti = lax.broadcasted_iota(jnp.int32, (c, c), 0)
        tj = lax.broadcasted_iota(jnp.int32, (c, c), 1)
        tri = jnp.where(ti >= tj, 1.0, 0.0).astype(BF16)

        def yrows(bi):
            return pl.ds(bi * tb + r0, c)

        def hgrn(bi):
            q_raw = p_scr[bi, rows, HG0:HG0 + 256]
            f_raw = p_scr[bi, rows, HG0 + 256:HG0 + 512]
            i_raw = p_scr[bi, rows, HG0 + 512:HG0 + 768]
            lb = vec(0)
            fg = lb + (1.0 - lb) * _sigmoid(f_raw)
            o = yield from _gla_core(_silu(q_raw), 1.0 - fg, i_raw, st_hg.at[bi], HD, tri,
                                     logf=jnp.log(fg))
            ms = _head_mean(o * o, hs_ref[...])
            yield
            g_raw = p_scr[bi, rows, HG0 + 768:HG0 + 1024]
            o = o * lax.rsqrt(ms + 1e-6) * vec(1) * _silu(g_raw)
            y_scr[yrows(bi), 0:256] = o.astype(BF16)

        def gla(bi):
            glr = p_scr[bi, rows, GL0 + 768:GL0 + 896]
            z = _dot_hl(glr, glaw2_ref)
            yield
            logf = -_softplus(-(z + vec(2, 128))) * (1.0 / GLA_GATE_NORM)
            q = p_scr[bi, rows, GL0:GL0 + 128] * (GLA_KEY ** -0.5)
            k = p_scr[bi, rows, GL0 + 128:GL0 + 256]
            v = p_scr[bi, rows, GL0 + 256:GL0 + 512]
            o = yield from _gla_core(q, k, v, st_gl.at[bi], GLA_KEY, tri, logf=logf)
            ms = _head_mean(o * o, hs_ref[...])
            yield
            rg = p_scr[bi, rows, GL0 + 512:GL0 + 768]
            o = o * lax.rsqrt(ms + 1e-6) * vec(3) * _silu(rg)
            y_scr[yrows(bi), 256:512] = o.astype(BF16)

        def rwkv(bi):
            hs = hs_ref[...]
            xs = p_scr[bi, pl.ds(r0, c + 8), RW0:RW0 + RWW]
            cur = xs[8:]
            prev = pltpu.roll(xs, 1, 0)[8:]
            pm = cur + (prev - cur) * mu_ref[0:1, :]
            r = pm[:, 0:256]
            k = pm[:, 256:512]
            v = pm[:, 512:768]
            wa = pm[:, 768:896]
            lane = lax.broadcasted_iota(jnp.int32, (1, 128), 1)
            pre = _dot_hl(jnp.where(lane < 64, jnp.tanh(wa), wa), w2a2_ref)
            gate = _dot(_sigmoid(pm[:, 896:1152]), g2_ref[...])
            kk = k * vec(6)
            kk_ms = _head_mean(kk * kk, hs)
            if first:
                vf_out_ref[bi, crow, :] = v
            else:
                t1 = _dot(v, v1_ref[...])
            yield
            w = -_softplus(-(vec(4) + pre[:, 0:256])) - 0.5
            logw = -jnp.exp(w)
            a = _sigmoid(vec(5) + pre[:, 256:512])
            kk = kk / jnp.maximum(jnp.sqrt(kk_ms * HD), 1e-12)
            k = k * (1.0 + (a - 1.0) * vec(7))
            rk = _head_mean(r * k * vec(8), hs)
            if not first:
                t2 = _dot(t1, v2_ref[...])
                yield
                vf = vf_in_ref[bi, crow, :]
                v = v + (vf - v) * _sigmoid(vec(11) + t2)
            o = yield from _rwkv_core(r, logw, k, v, kk, a, st_rw.at[bi], tri)
            mu_o = _head_mean(o, hs)
            yield
            oc = o - mu_o
            var = _head_mean(oc * oc, hs)
            yield
            o = oc * lax.rsqrt(var + RWKV_LNX_EPS) * vec(9) + vec(10)
            o = o + (rk * HD) * v
            y_scr[yrows(bi), 512:768] = (o * gate).astype(BF16)

        def retnet(bi):
            hs = hs_ref[...]
            q = p_scr[bi, rows, RT0:RT0 + 256]
            k = p_scr[bi, rows, RT0 + 256:RT0 + 512]
            v = p_scr[bi, rows, RT0 + 512:RT0 + 768]
            cs = cos_ref[bi, crow, :]
            sn = sin_ref[bi, crow, :]
            cs = jnp.concatenate([cs, cs], axis=1)
            sn = jnp.concatenate([sn, sn], axis=1)
            lane_r = lax.broadcasted_iota(jnp.int32, (1, 256), 1)
            first_half = (lane_r & (HD - 1)) < (HD // 2)

            def rot(x):
                swapped = jnp.where(first_half, pltpu.roll(x, 256 - HD // 2, 1),
                                    pltpu.roll(x, HD // 2, 1))
                return x * cs + swapped * sn

            factors = (rfac_ref[0], rfac_ref[1], rfac_ref[2], rfac_ref[3], rfac_ref[4, 0:1, :])
            o = yield from _gla_core(rot(q), rot(k) * (HD ** -0.5), v, st_rt.at[bi], HD, tri,
                                     factors=factors)
            mu_o = _head_mean(o, hs)
            yield
            oc = o - mu_o
            var = _head_mean(oc * oc, hs)
            yield
            gr = p_scr[bi, rows, RT0 + 768:RT0 + 1024]
            y_scr[yrows(bi), 768:1024] = (_silu(gr) * oc * lax.rsqrt(var + LN_EPS)).astype(BF16)

        gens = [rwkv(bi) for bi in range(nb)]
        for bi in range(nb):
            gens += [hgrn(bi), gla(bi), retnet(bi)]
        _run_interleaved(gens)
        return carry

    lax.fori_loop(0, tb // c, chunk_body, 0)

    mix = jnp.dot(y_scr[...], wout_ref[...], preferred_element_type=F32)
    hres = DN_ALPHA * h_ref[...].reshape(nb * tb, D_MODEL) + mix
    out_ref[...] = _layer_norm(hres, ln_ref[0:1, :], ln_ref[1:2, :]).reshape(nb, tb, D_MODEL)


def _mixer_call(first, h, cos_t, sin_t, w_in, w_out, vecs, mu, ln, glaw2, w2a2, g2, hs, rfac,
                vf=None, v1=None, v2=None):
    b, s, d = h.shape
    tb = TB_MIX
    nb = b
    grid = (s // tb,)

    def tok(width):
        return pl.BlockSpec((nb, tb, width), lambda j: (0, j, 0))

    def full(arr):
        nd = arr.ndim
        return pl.BlockSpec(arr.shape, lambda j: (0,) * nd)

    args = [h, cos_t, sin_t, w_in, w_out, vecs, mu, ln, glaw2, w2a2, g2, hs, rfac]
    in_specs = [tok(d), tok(128), tok(128)] + [full(a) for a in args[3:]]
    out_h = jax.ShapeDtypeStruct((b, s, d), F32)
    if first:
        out_shape = (out_h, jax.ShapeDtypeStruct((b, s, GROUP), F32))
        out_specs = (tok(d), tok(GROUP))
    else:
        args += [vf, v1, v2]
        in_specs += [tok(GROUP), full(v1), full(v2)]
        out_shape = out_h
        out_specs = tok(d)
    scratch = [
        pltpu.VMEM((nb, tb + 8, IN_PAD), F32),
        pltpu.VMEM((nb * tb, d), BF16),
        pltpu.VMEM((nb, GROUP, GROUP), F32),
        pltpu.VMEM((nb, GROUP, NH * GLA_KEY), F32),
        pltpu.VMEM((nb, GROUP, GROUP), F32),
        pltpu.VMEM((nb, GROUP, GROUP), F32),
    ]
    return pl.pallas_call(
        functools.partial(_mixer_kernel, first, nb, tb),
        grid=grid,
        in_specs=in_specs,
        out_specs=out_specs,
        out_shape=out_shape,
        scratch_shapes=scratch,
        compiler_params=pltpu.CompilerParams(
            dimension_semantics=("arbitrary",), vmem_limit_bytes=VMEM_LIMIT),
        name="token_mixer_first" if first else "token_mixer",
    )(*args)


def _ffn_kernel(h_ref, wg_ref, wu_ref, wd_ref, ln_ref, out_ref, hb_scr, acc_scr):
    j = pl.program_id(1)

    @pl.when(j == 0)
    def _():
        hb_scr[...] = h_ref[...].astype(BF16)
        acc_scr[...] = jnp.zeros_like(acc_scr)

    hb = hb_scr[...]
    g = jnp.dot(hb, wg_ref[...], preferred_element_type=F32)
    u = jnp.dot(hb, wu_ref[...], preferred_element_type=F32)
    act = (_silu(g) * u).astype(BF16)
    acc_scr[...] += jnp.dot(act, wd_ref[...], preferred_element_type=F32)

    @pl.when(j == pl.num_programs(1) - 1)
    def _():
        out_ref[...] = _layer_norm(DN_ALPHA * h_ref[...] + acc_scr[...], ln_ref[0:1, :], ln_ref[1:2, :])


def _ffn_call(h2, wg, wu, wd, ln):
    t, d = h2.shape
    f = wg.shape[1]
    tm, fb = TM_FFN, FB_FFN
    return pl.pallas_call(
        _ffn_kernel,
        grid=(t // tm, f // fb),
        in_specs=[
            pl.BlockSpec((tm, d), lambda i, j: (i, 0)),
            pl.BlockSpec((d, fb), lambda i, j: (0, j)),
            pl.BlockSpec((d, fb), lambda i, j: (0, j)),
            pl.BlockSpec((fb, d), lambda i, j: (j, 0)),
            pl.BlockSpec(ln.shape, lambda i, j: (0, 0)),
        ],
        out_specs=pl.BlockSpec((tm, d), lambda i, j: (i, 0)),
        out_shape=jax.ShapeDtypeStruct((t, d), F32),
        scratch_shapes=[pltpu.VMEM((tm, d), BF16), pltpu.VMEM((tm, d), F32)],
        compiler_params=pltpu.CompilerParams(
            dimension_semantics=("arbitrary", "arbitrary"), vmem_limit_bytes=VMEM_LIMIT),
        name="ffn_dense",
    )(h2, wg, wu, wd, ln)


def _moe_kernel(h_ref, rt_ref, tri_ref, wg_ref, wu_ref, wd_ref, ln_ref, out_ref,
                hb_scr, col_scr, row_scr, acc_scr):
    e = pl.program_id(1)
    nsub = NSUB_MOE
    sub = h_ref.shape[0] // nsub
    csub = CSUB_MOE
    ne = N_EXPERTS

    @pl.when(e == 0)
    def _():
        h = h_ref[...]
        hb_scr[...] = h.astype(BF16)
        acc_scr[...] = jnp.zeros_like(acc_scr)
        logits = _dot_hl(h, rt_ref)
        lane = lax.broadcasted_iota(jnp.int32, logits.shape, 1)
        neg = jnp.float32(-jnp.inf)
        logits = jnp.where(lane < ne, logits, neg)
        m1 = jnp.max(logits, axis=-1, keepdims=True)
        i1 = jnp.min(jnp.where(logits == m1, lane, 128), axis=-1, keepdims=True)
        rest = jnp.where(lane == i1, neg, logits)
        m2 = jnp.max(rest, axis=-1, keepdims=True)
        i2 = jnp.min(jnp.where(rest == m2, lane, 128), axis=-1, keepdims=True)
        e2 = jnp.exp(m2 - m1)
        g1 = 1.0 / (1.0 + e2)
        g2 = e2 / (1.0 + e2)
        sel = (lane == i1) | (lane == i2)
        gates = jnp.where(lane == i1, g1, 0.0) + jnp.where(lane == i2, g2, 0.0)
        selb = jnp.where(sel, 1.0, 0.0).astype(BF16)
        tri = tri_ref[...]
        rank = jnp.concatenate(
            [jnp.dot(tri, selb[j * sub:(j + 1) * sub], preferred_element_type=F32) for j in range(nsub)],
            axis=0)
        slot = jnp.where(sel, rank, -1.0)
        packed = jnp.where(lane < ne, gates, pltpu.roll(slot, ne, 1))
        col_scr[...] = packed
        row_scr[...] = packed.T

    slot_row = row_scr[pl.ds(ne + e, 1), :]
    hit = jnp.where(slot_row >= 0.0, 1.0, 0.0)
    n_max = jnp.sum(hit[:, 0:sub])
    for j in range(1, nsub):
        n_max = jnp.maximum(n_max, jnp.sum(hit[:, j * sub:(j + 1) * sub]))
    n_tiles = (n_max.astype(jnp.int32) + (csub - 1)) // csub
    col = col_scr[...]
    lane = lax.broadcasted_iota(jnp.int32, col.shape, 1)
    slot_col = jnp.sum(jnp.where(lane == ne + e, col, 0.0), axis=-1, keepdims=True)
    gate_col = jnp.sum(jnp.where(lane == e, col, 0.0), axis=-1, keepdims=True)

    def tile(s, carry):
        base = (s * csub).astype(F32)
        jr = lax.broadcasted_iota(jnp.int32, (csub, sub), 0).astype(F32) + base
        xs = []
        for j in range(nsub):
            gather = jnp.where(jr == slot_row[:, j * sub:(j + 1) * sub], 1.0, 0.0).astype(BF16)
            xs.append(jnp.dot(gather, hb_scr[j * sub:(j + 1) * sub, :],
                              preferred_element_type=F32).astype(BF16))
        x = jnp.concatenate(xs, axis=0)
        g = jnp.dot(x, wg_ref[...], preferred_element_type=F32)
        u = jnp.dot(x, wu_ref[...], preferred_element_type=F32)
        act = (_silu(g) * u).astype(BF16)
        y = jnp.dot(act, wd_ref[...], preferred_element_type=F32).astype(BF16)
        jc = lax.broadcasted_iota(jnp.int32, (sub, csub), 1).astype(F32) + base
        for j in range(nsub):
            rows = slice(j * sub, (j + 1) * sub)
            scatter = jnp.where(jc == slot_col[rows], gate_col[rows], 0.0).astype(BF16)
            acc_scr[rows, :] += jnp.dot(scatter, y[j * csub:(j + 1) * csub],
                                        preferred_element_type=F32)
        return carry

    lax.fori_loop(0, n_tiles, tile, 0)

    @pl.when(e == pl.num_programs(1) - 1)
    def _():
        out_ref[...] = _layer_norm(DN_ALPHA * h_ref[...] + acc_scr[...], ln_ref[0:1, :], ln_ref[1:2, :])


def _moe_call(h2, router_hl, wg, wu, wd, ln):
    t, d = h2.shape
    ne, _, f = wg.shape
    tm = TM_MOE
    sub = tm // NSUB_MOE
    ti = jnp.arange(sub)
    tri = (ti[:, None] > ti[None, :]).astype(BF16)
    return pl.pallas_call(
        _moe_kernel,
        grid=(t // tm, ne),
        in_specs=[
            pl.BlockSpec((tm, d), lambda i, e: (i, 0)),
            pl.BlockSpec(router_hl.shape, lambda i, e: (0, 0, 0)),
            pl.BlockSpec((sub, sub), lambda i, e: (0, 0)),
            pl.BlockSpec((None, d, f), lambda i, e: (e, 0, 0)),
            pl.BlockSpec((None, d, f), lambda i, e: (e, 0, 0)),
            pl.BlockSpec((None, f, d), lambda i, e: (e, 0, 0)),
            pl.BlockSpec(ln.shape, lambda i, e: (0, 0)),
        ],
        out_specs=pl.BlockSpec((tm, d), lambda i, e: (i, 0)),
        out_shape=jax.ShapeDtypeStruct((t, d), F32),
        scratch_shapes=[pltpu.VMEM((tm, d), BF16), pltpu.VMEM((tm, 128), F32),
                        pltpu.VMEM((128, tm), F32), pltpu.VMEM((tm, d), F32)],
        compiler_params=pltpu.CompilerParams(
            dimension_semantics=("arbitrary", "arbitrary"), vmem_limit_bytes=VMEM_LIMIT),
        name="ffn_experts",
    )(h2, router_hl, tri, wg, wu, wd, ln)


def _trig_kernel(ang_ref, cos_ref, sin_ref):
    ang = ang_ref[...]
    cos_ref[...] = jnp.cos(ang)
    sin_ref[...] = jnp.sin(ang)


def _rotary_tables(positions):
    b, s = positions.shape
    half = HD // 2
    inv_freq = ROPE_BASE ** (-jnp.arange(half, dtype=F32) / half)
    ang = positions.astype(F32)[:, :, None] * inv_freq
    rows = b * s * half // 128
    ang2 = ang.reshape(rows, 128)
    blk = min(rows, 1024)
    cos2, sin2 = pl.pallas_call(
        _trig_kernel,
        grid=(rows // blk,),
        in_specs=[pl.BlockSpec((blk, 128), lambda i: (i, 0))],
        out_specs=(pl.BlockSpec((blk, 128), lambda i: (i, 0)),) * 2,
        out_shape=(jax.ShapeDtypeStruct((rows, 128), F32),) * 2,
        name="rotary_tables",
    )(ang2)
    cos = cos2.reshape(b, s, half)
    sin = sin2.reshape(b, s, half)
    cos_t = jnp.concatenate([cos, cos, cos, cos], axis=-1)
    sin_t = jnp.concatenate([-sin, sin, -sin, sin], axis=-1)
    return cos_t, sin_t


def _hl(w):
    hi = w.astype(BF16)
    lo = (w - hi.astype(F32)).astype(BF16)
    return jnp.stack([hi, lo])


def _pad_to(a, shape):
    return jnp.pad(a, [(0, t - s) for s, t in zip(a.shape, shape)])


def _pack_w_in(w):
    d = w.shape[0]
    z = lambda n: jnp.zeros((d, n), w.dtype)
    gl, rw, rt = 1024, 1808, 2864
    cols = [
        w[:, 0:1024],
        w[:, gl:gl + 512], w[:, gl + 528:gl + 784], w[:, gl + 512:gl + 528], z(112),
        w[:, rw:rw + 1056], z(96),
        w[:, rt:rt + 1024],
    ]
    return jnp.concatenate(cols, axis=1).astype(BF16)


def kernel(x, positions, w_in, w_out, ln1_g, ln1_b, ln2_g, ln2_b, hgrn_lb_logits, hgrn_norm_g,
           gla_gate_w2, gla_gate_b, gla_norm_g, rwkv_mu, rwkv_w0, rwkv_w2, rwkv_a0, rwkv_a2,
           rwkv_g2, rwkv_k_k, rwkv_k_a, rwkv_r_k, rwkv_lnx_g, rwkv_lnx_b, rwkv_v0, rwkv_v1,
           rwkv_v2, ffn_w_gate, ffn_w_up, ffn_w_down, moe_router, moe_w_gate, moe_w_up,
           moe_w_down):
    out_dtype = x.dtype
    b, s, d = x.shape
    h = x.astype(F32)
    cos_t, sin_t = _rotary_tables(positions)

    lb_all = jnp.cumsum(jax.nn.softmax(hgrn_lb_logits.astype(F32), axis=0), axis=0)
    lb_all = lb_all - lb_all[:1]
    log_gamma = jnp.log(1.0 - 2.0 ** (-5.0 - jnp.arange(NH, dtype=F32)))
    log_gamma_row = jnp.repeat(log_gamma, HD)
    head_i = jnp.arange(GROUP) // HD
    hs = (head_i[:, None] == head_i[None, :]).astype(BF16)
    tile4 = lambda v: jnp.tile(v, NH)
    cum = (jnp.arange(CH, dtype=F32)[:, None] + 1.0) * log_gamma_row[None, :]
    mid, cend = cum[CH // 2:CH // 2 + 1], cum[CH - 1:CH]
    rfac = jnp.stack([jnp.exp(cum), jnp.exp(cum - mid), jnp.exp(mid - cum), jnp.exp(cend - cum),
                      jnp.broadcast_to(jnp.exp(cend), cum.shape)])
    zrow = jnp.zeros((GROUP,), F32)

    v_first = None
    for l in range(DEPTH):
        rows = [
            lb_all[l], tile4(hgrn_norm_g[l]), _pad_to(gla_gate_b[l], (GROUP,)), tile4(gla_norm_g[l]),
            rwkv_w0[l], rwkv_a0[l], rwkv_k_k[l], rwkv_k_a[l], rwkv_r_k[l].reshape(GROUP),
            rwkv_lnx_g[l], rwkv_lnx_b[l], rwkv_v0[l - 1] if l > 0 else zrow, log_gamma_row,
            zrow, zrow, zrow,
        ]
        vecs = jnp.stack(rows).astype(F32)
        mu = _pad_to(rwkv_mu[l][None, :], (8, RWW))
        ln1 = _pad_to(jnp.stack([ln1_g[l], ln1_b[l]]), (8, d))
        ln2 = _pad_to(jnp.stack([ln2_g[l], ln2_b[l]]), (8, d))
        glaw2 = _hl(_pad_to(gla_gate_w2[l], (128, 128)))
        w2a2 = jnp.zeros((128, 512), F32)
        w2a2 = w2a2.at[0:64, 0:256].set(rwkv_w2[l]).at[64:128, 256:512].set(rwkv_a2[l])
        w2a2 = _hl(w2a2)
        g2 = _pad_to(rwkv_g2[l], (256, 256)).astype(BF16)
        win = _pack_w_in(w_in[l])
        wout = w_out[l].astype(BF16)
        if l == 0:
            h, v_first = _mixer_call(True, h, cos_t, sin_t, win, wout, vecs, mu, ln1, glaw2, w2a2,
                                     g2, hs, rfac)
        else:
            v1 = _pad_to(rwkv_v1[l - 1], (256, 128)).astype(BF16)
            v2 = _pad_to(rwkv_v2[l - 1], (128, 256)).astype(BF16)
            h = _mixer_call(False, h, cos_t, sin_t, win, wout, vecs, mu, ln1, glaw2, w2a2, g2, hs,
                            rfac, v_first, v1, v2)
        h2 = h.reshape(b * s, d)
        i = l // 2
        if l % 2 == 0:
            h2 = _ffn_call(h2, ffn_w_gate[i].astype(BF16), ffn_w_up[i].astype(BF16),
                           ffn_w_down[i].astype(BF16), ln2)
        else:
            router = _hl(_pad_to(moe_router[i], (d, 128)))
            h2 = _moe_call(h2, router, moe_w_gate[i].astype(BF16), moe_w_up[i].astype(BF16),
                           moe_w_down[i].astype(BF16), ln2)
        h = h2.reshape(b, s, d)
    return h.astype(out_dtype)
```

```python
import functools

import jax
import jax.numpy as jnp
from jax import lax
from jax.experimental import pallas as pl
from jax.experimental.pallas import tpu as pltpu

F32 = jnp.float32
BF16 = jnp.bfloat16

D_MODEL = 1024
DEPTH = 4
GROUP = 256
NH = 4
HD = 64
CH = HD
GLA_KEY = 32
GLA_GATE_NORM = 16.0
RWKV_LNX_EPS = 64e-5
ROPE_BASE = 10000.0
FFN_DENSE = 2816
N_EXPERTS = 8
FFN_EXPERT = 1408
DN_ALPHA = (2.0 * DEPTH) ** 0.25
LN_EPS = 1e-5

HG0 = 0
GL0 = 1024
RW0 = 1920
RWW = 1152
RT0 = 3072
IN_PAD = 4096

TB_MIX = 128
TM_FFN = 1024
FB_FFN = 256
TM_MOE = 1024
NSUB_MOE = 2
CSUB_MOE = 160
VMEM_LIMIT = 56 * 1024 * 1024


def _dot(a, b):
    return jnp.dot(a.astype(BF16), b.astype(BF16), preferred_element_type=F32)


def _dot_nt(a, b):
    return lax.dot_general(a.astype(BF16), b.astype(BF16), (((1,), (1,)), ((), ())),
                           preferred_element_type=F32)


def _dot_tn(a, b):
    return lax.dot_general(a.astype(BF16), b.astype(BF16), (((0,), (0,)), ((), ())),
                           preferred_element_type=F32)


def _split2(x):
    hi = x.astype(BF16)
    lo = (x - hi.astype(F32)).astype(BF16)
    return hi, lo


def _dot_hl(a, w_ref):
    a_hi, a_lo = _split2(a)
    w_hi = w_ref[0]
    w_lo = w_ref[1]
    return (jnp.dot(a_hi, w_hi, preferred_element_type=F32)
            + jnp.dot(a_lo, w_hi, preferred_element_type=F32)
            + jnp.dot(a_hi, w_lo, preferred_element_type=F32))


def _sigmoid(x):
    return 1.0 / (1.0 + jnp.exp(-x))


def _silu(x):
    return x * _sigmoid(x)


def _softplus(x):
    return jnp.maximum(x, 0.0) + jnp.log(1.0 + jnp.exp(-jnp.abs(x)))


def _layer_norm(x, g, b):
    mu = jnp.mean(x, axis=-1, keepdims=True)
    xc = x - mu
    var = jnp.mean(xc * xc, axis=-1, keepdims=True)
    return xc * lax.rsqrt(var + LN_EPS) * g + b


def _cumsum_rows(x, tri):
    hi, lo = _split2(x)
    return jnp.dot(tri, hi, preferred_element_type=F32) + jnp.dot(tri, lo, preferred_element_type=F32)


def _head_mean(x, hs):
    return jnp.dot(x.astype(BF16), hs, preferred_element_type=F32) * (1.0 / HD)


def _stack_heads(x, group):
    lane = lax.broadcasted_iota(jnp.int32, (1, x.shape[1]), 1)
    parts = [jnp.where((lane // group) == h, x, 0.0) for h in range(NH)]
    return jnp.concatenate(parts, axis=0)


def _run_interleaved(gens):
    alive = list(gens)
    while alive:
        still = []
        for g in alive:
            try:
                next(g)
                still.append(g)
            except StopIteration:
                pass
        alive = still


def _lockstep(gens):
    results = [None] * len(gens)
    alive = list(range(len(gens)))
    while alive:
        still = []
        for i in alive:
            try:
                next(gens[i])
                still.append(i)
            except StopIteration as stop:
                results[i] = stop.value
        alive = still
        if alive:
            yield
    return results


def _gla_core(q, k, v, st_ref, kgroup, tri, logf=None, factors=None):
    c = q.shape[0]
    kt = q.shape[1]
    if factors is None:
        cum = _cumsum_rows(logf, tri)
        yield
        mid = cum[c // 2:c // 2 + 1]
        cend = cum[c - 1:c]
        e_i = jnp.exp(cum)
        e_a = jnp.exp(cum - mid)
        e_k = jnp.exp(mid - cum)
        e_s = jnp.exp(cend - cum)
        e_end = jnp.exp(cend)
    else:
        e_i, e_a, e_k, e_s, e_end = factors
    a = _dot_nt(q * e_a, _stack_heads(k * e_k, kgroup))
    st = st_ref[...]
    o_int = _dot_nt(q * e_i, st)
    upd = _dot_tn(v, k * e_s)
    yield
    row = lax.broadcasted_iota(jnp.int32, a.shape, 0)
    col = lax.broadcasted_iota(jnp.int32, a.shape, 1) & (c - 1)
    a = jnp.where(row >= col, a, 0.0).astype(BF16)
    r_h = lax.broadcasted_iota(jnp.int32, (NH * HD, kt), 0) // HD
    c_h = lax.broadcasted_iota(jnp.int32, (NH * HD, kt), 1) // kgroup
    st_ref[...] = st * e_end + jnp.where(r_h == c_h, upd, 0.0)
    o_intra = jnp.dot(a, _stack_heads(v, HD).astype(BF16), preferred_element_type=F32)
    yield
    return o_int + o_intra


def _rwkv_core(r, logw, k, v, kk, a, st_ref, tri):
    c = r.shape[0]
    n = NH * c
    cum = _cumsum_rows(logw, tri)
    yield
    cump = cum - logw
    mid = cum[c // 2:c // 2 + 1]
    cend = cum[c - 1:c]
    eq = jnp.exp(cum - mid)
    eqp = jnp.exp(cump - mid)
    ek = jnp.exp(mid - cum)
    es = jnp.exp(cend - cum)
    ei = jnp.exp(cum)
    eip = jnp.exp(cump)
    b = kk * a
    na = -kk

    def stack(x):
        return _stack_heads(x, HD).astype(BF16)

    lhs = jnp.concatenate([na * eqp, r * eq], axis=0).astype(BF16)
    rhs = jnp.concatenate([stack(b * ek), stack(k * ek)], axis=0)
    sc = lax.dot_general(lhs, rhs, (((1,), (1,)), ((), ())), preferred_element_type=F32)
    st = st_ref[...]
    stb = st.astype(BF16)
    rs = _dot_nt(jnp.concatenate([na * eip, r * ei], axis=0), stb)
    rs_u = rs[:c]
    rs_o = rs[c:]
    vs = stack(v)
    yield
    i0 = lax.broadcasted_iota(jnp.int32, (c, n), 0)
    i1 = lax.broadcasted_iota(jnp.int32, (c, n), 1) & (c - 1)
    strict = i0 > i1
    incl = i0 >= i1
    a_ab = jnp.where(strict, sc[:c, :n], 0.0)
    a_ak = jnp.where(strict, sc[:c, n:], 0.0).astype(BF16)
    a_rb = jnp.where(incl, sc[c:, :n], 0.0).astype(BF16)
    a_rk = jnp.where(incl, sc[c:, n:], 0.0).astype(BF16)
    same16 = (i0 >> 4) == (i1 >> 4)
    eye = jnp.where(i0 == i1, 1.0, 0.0).astype(F32)
    dm = jnp.where(same16, a_ab, 0.0)
    em = a_ab - dm
    p = eye + dm
    d2 = _dot(dm, stack(dm))
    av = jnp.dot(jnp.concatenate([a_ak, a_rk], axis=0), vs, preferred_element_type=F32)
    ak_v = av[:c]
    rk_v = av[c:]
    yield
    pd = _dot(jnp.concatenate([p, d2], axis=0), stack(d2))
    yield
    p = p + pd[:c]
    d4 = pd[c:]
    pd = _dot(jnp.concatenate([p, d4], axis=0), stack(d4))
    yield
    p = p + pd[:c]
    d8 = pd[c:]
    pd = _dot(p, stack(d8))
    yield
    p = p + pd
    f = _dot(p, stack(em))
    yield
    f2 = _dot(f, stack(f))
    yield
    ff2 = _dot(f, stack(f2))
    yield
    g = eye + f + f2 + ff2
    t = _dot(g, stack(p))
    yield
    u = _dot(t, stack(rs_u + ak_v))
    yield
    rb_u = jnp.dot(a_rb, stack(u), preferred_element_type=F32)
    s_new = _dot_tn(jnp.concatenate([u, v], axis=0), jnp.concatenate([b * es, k * es], axis=0))
    yield
    r_h = lax.broadcasted_iota(jnp.int32, s_new.shape, 0) // HD
    c_h = lax.broadcasted_iota(jnp.int32, s_new.shape, 1) // HD
    st_ref[...] = st * jnp.exp(cend) + jnp.where(r_h == c_h, s_new, 0.0)
    return rs_o + rb_u + rk_v


def _mixer_kernel(first, nb, tb, *refs):
    if first:
        (h_ref, cos_ref, sin_ref, win_ref, wout_ref, vec_ref, mu_ref, ln_ref, glaw2_ref, w2a2_ref,
         g2_ref, hs_ref, rfac_ref, out_ref, vf_out_ref, p_scr, y_scr, st_hg, st_gl, st_rw,
         st_rt) = refs
        vf_in_ref = v1_ref = v2_ref = None
    else:
        (h_ref, cos_ref, sin_ref, win_ref, wout_ref, vec_ref, mu_ref, ln_ref, glaw2_ref, w2a2_ref,
         g2_ref, hs_ref, rfac_ref, vf_in_ref, v1_ref, v2_ref, out_ref, p_scr, y_scr, st_hg, st_gl,
         st_rw, st_rt) = refs
        vf_out_ref = None
    c = CH
    t_idx = pl.program_id(0)

    @pl.when(t_idx == 0)
    def _():
        st_hg[...] = jnp.zeros_like(st_hg)
        st_gl[...] = jnp.zeros_like(st_gl)
        st_rw[...] = jnp.zeros_like(st_rw)
        st_rt[...] = jnp.zeros_like(st_rt)
        p_scr[:, 0:8, RW0:RW0 + RWW] = jnp.zeros((nb, 8, RWW), F32)

    @pl.when(t_idx > 0)
    def _():
        p_scr[:, 0:8, RW0:RW0 + RWW] = p_scr[:, tb:tb + 8, RW0:RW0 + RWW]

    hb = h_ref[...].reshape(nb * tb, D_MODEL).astype(BF16)
    for j in range(IN_PAD // 1024):
        res = jnp.dot(hb, win_ref[:, j * 1024:(j + 1) * 1024], preferred_element_type=F32)
        for bi in range(nb):
            p_scr[bi, 8:tb + 8, j * 1024:(j + 1) * 1024] = res[bi * tb:(bi + 1) * tb]

    def vec(i, n=GROUP):
        return vec_ref[i:i + 1, 0:n]

    def chunk_body(ci, carry):
        r0 = pl.multiple_of(ci * c, c)
        rows = pl.ds(r0 + 8, c)
        crow = pl.ds(r0, c)
        ti = lax.broadcasted_iota(jnp.int32, (c, c), 0)
        tj = lax.broadcasted_iota(jnp.int32, (c, c), 1)
        tri = jnp.where(ti >= tj, 1.0, 0.0).astype(BF16)

        def ld(c0, c1):
            return jnp.concatenate([p_scr[bi, rows, c0:c1] for bi in range(nb)], axis=0)

        def sl(x, bi):
            return x[bi * c:(bi + 1) * c]

        def store_y(c0, val):
            for bi in range(nb):
                y_scr[pl.ds(bi * tb + r0, c), c0:c0 + GROUP] = sl(val, bi).astype(BF16)

        def hgrn():
            q = _silu(ld(HG0, HG0 + 256))
            lb = vec(0)
            fg = lb + (1.0 - lb) * _sigmoid(ld(HG0 + 256, HG0 + 512))
            k = 1.0 - fg
            logf = jnp.log(fg)
            v = ld(HG0 + 512, HG0 + 768)
            outs = yield from _lockstep([
                _gla_core(sl(q, bi), sl(k, bi), sl(v, bi), st_hg.at[bi], HD, tri, logf=sl(logf, bi))
                for bi in range(nb)])
            o = jnp.concatenate(outs, axis=0)
            ms = _head_mean(o * o, hs_ref[...])
            yield
            o = o * lax.rsqrt(ms + 1e-6) * vec(1) * _silu(ld(HG0 + 768, HG0 + 1024))
            store_y(0, o)

        def gla():
            z = _dot_hl(ld(GL0 + 768, GL0 + 896), glaw2_ref)
            yield
            logf = -_softplus(-(z + vec(2, 128))) * (1.0 / GLA_GATE_NORM)
            q = ld(GL0, GL0 + 128) * (GLA_KEY ** -0.5)
            k = ld(GL0 + 128, GL0 + 256)
            v = ld(GL0 + 256, GL0 + 512)
            outs = yield from _lockstep([
                _gla_core(sl(q, bi), sl(k, bi), sl(v, bi), st_gl.at[bi], GLA_KEY, tri,
                          logf=sl(logf, bi))
                for bi in range(nb)])
            o = jnp.concatenate(outs, axis=0)
            ms = _head_mean(o * o, hs_ref[...])
            yield
            o = o * lax.rsqrt(ms + 1e-6) * vec(3) * _silu(ld(GL0 + 512, GL0 + 768))
            store_y(256, o)

        def rwkv():
            hs = hs_ref[...]
            curs, prevs = [], []
            for bi in range(nb):
                xs = p_scr[bi, pl.ds(r0, c + 8), RW0:RW0 + RWW]
                curs.append(xs[8:])
                prevs.append(pltpu.roll(xs, 1, 0)[8:])
            cur = jnp.concatenate(curs, axis=0)
            prev = jnp.concatenate(prevs, axis=0)
            pm = cur + (prev - cur) * mu_ref[0:1, :]
            r = pm[:, 0:256]
            k = pm[:, 256:512]
            v = pm[:, 512:768]
            wa = pm[:, 768:896]
            lane = lax.broadcasted_iota(jnp.int32, (1, 128), 1)
            pre = _dot_hl(jnp.where(lane < 64, jnp.tanh(wa), wa), w2a2_ref)
            gate = _dot(_sigmoid(pm[:, 896:1152]), g2_ref[...])
            kk = k * vec(6)
            kk_ms = _head_mean(kk * kk, hs)
            if first:
                for bi in range(nb):
                    vf_out_ref[bi, crow, :] = sl(v, bi)
            else:
                t1 = _dot(v, v1_ref[...])
            yield
            w = -_softplus(-(vec(4) + pre[:, 0:256])) - 0.5
            logw = -jnp.exp(w)
            a = _sigmoid(vec(5) + pre[:, 256:512])
            kk = kk / jnp.maximum(jnp.sqrt(kk_ms * HD), 1e-12)
            k = k * (1.0 + (a - 1.0) * vec(7))
            rk = _head_mean(r * k * vec(8), hs)
            if not first:
                t2 = _dot(t1, v2_ref[...])
                yield
                vf = jnp.concatenate([vf_in_ref[bi, crow, :] for bi in range(nb)], axis=0)
                v = v + (vf - v) * _sigmoid(vec(11) + t2)
            outs = yield from _lockstep([
                _rwkv_core(sl(r, bi), sl(logw, bi), sl(k, bi), sl(v, bi), sl(kk, bi), sl(a, bi),
                           st_rw.at[bi], tri)
                for bi in range(nb)])
            o = jnp.concatenate(outs, axis=0)
            mu_o = _head_mean(o, hs)
            yield
            oc = o - mu_o
            var = _head_mean(oc * oc, hs)
            yield
            o = oc * lax.rsqrt(var + RWKV_LNX_EPS) * vec(9) + vec(10)
            o = o + (rk * HD) * v
            store_y(512, o * gate)

        def retnet():
            hs = hs_ref[...]
            q = ld(RT0, RT0 + 256)
            k = ld(RT0 + 256, RT0 + 512)
            v = ld(RT0 + 512, RT0 + 768)
            cs = jnp.concatenate([cos_ref[bi, crow, :] for bi in range(nb)], axis=0)
            sn = jnp.concatenate([sin_ref[bi, crow, :] for bi in range(nb)], axis=0)
            cs = jnp.concatenate([cs, cs], axis=1)
            sn = jnp.concatenate([sn, sn], axis=1)
            lane_r = lax.broadcasted_iota(jnp.int32, (1, 256), 1)
            first_half = (lane_r & (HD - 1)) < (HD // 2)

            def rot(x):
                swapped = jnp.where(first_half, pltpu.roll(x, 256 - HD // 2, 1),
                                    pltpu.roll(x, HD // 2, 1))
                return x * cs + swapped * sn

            q = rot(q)
            k = rot(k) * (HD ** -0.5)
            factors = (rfac_ref[0], rfac_ref[1], rfac_ref[2], rfac_ref[3], rfac_ref[4, 0:1, :])
            outs = yield from _lockstep([
                _gla_core(sl(q, bi), sl(k, bi), sl(v, bi), st_rt.at[bi], HD, tri, factors=factors)
                for bi in range(nb)])
            o = jnp.concatenate(outs, axis=0)
            mu_o = _head_mean(o, hs)
            yield
            oc = o - mu_o
            var = _head_mean(oc * oc, hs)
            yield
            store_y(768, _silu(ld(RT0 + 768, RT0 + 1024)) * oc * lax.rsqrt(var + LN_EPS))

        _run_interleaved([rwkv(), hgrn(), gla(), retnet()])
        return carry

    lax.fori_loop(0, tb // c, chunk_body, 0)

    mix = jnp.dot(y_scr[...], wout_ref[...], preferred_element_type=F32)
    hres = DN_ALPHA * h_ref[...].reshape(nb * tb, D_MODEL) + mix
    out_ref[...] = _layer_norm(hres, ln_ref[0:1, :], ln_ref[1:2, :]).reshape(nb, tb, D_MODEL)


def _mixer_call(first, h, cos_t, sin_t, w_in, w_out, vecs, mu, ln, glaw2, w2a2, g2, hs, rfac,
                vf=None, v1=None, v2=None):
    b, s, d = h.shape
    tb = TB_MIX
    nb = b
    grid = (s // tb,)

    def tok(width):
        return pl.BlockSpec((nb, tb, width), lambda j: (0, j, 0))

    def full(arr):
        nd = arr.ndim
        return pl.BlockSpec(arr.shape, lambda j: (0,) * nd)

    args = [h, cos_t, sin_t, w_in, w_out, vecs, mu, ln, glaw2, w2a2, g2, hs, rfac]
    in_specs = [tok(d), tok(128), tok(128)] + [full(a) for a in args[3:]]
    out_h = jax.ShapeDtypeStruct((b, s, d), F32)
    if first:
        out_shape = (out_h, jax.ShapeDtypeStruct((b, s, GROUP), F32))
        out_specs = (tok(d), tok(GROUP))
    else:
        args += [vf, v1, v2]
        in_specs += [tok(GROUP), full(v1), full(v2)]
        out_shape = out_h
        out_specs = tok(d)
    scratch = [
        pltpu.VMEM((nb, tb + 8, IN_PAD), F32),
        pltpu.VMEM((nb * tb, d), BF16),
        pltpu.VMEM((nb, GROUP, GROUP), F32),
        pltpu.VMEM((nb, GROUP, NH * GLA_KEY), F32),
        pltpu.VMEM((nb, GROUP, GROUP), F32),
        pltpu.VMEM((nb, GROUP, GROUP), F32),
    ]
    return pl.pallas_call(
        functools.partial(_mixer_kernel, first, nb, tb),
        grid=grid,
        in_specs=in_specs,
        out_specs=out_specs,
        out_shape=out_shape,
        scratch_shapes=scratch,
        compiler_params=pltpu.CompilerParams(
            dimension_semantics=("arbitrary",), vmem_limit_bytes=VMEM_LIMIT),
        name="token_mixer_first" if first else "token_mixer",
    )(*args)


def _ffn_kernel(h_ref, wg_ref, wu_ref, wd_ref, ln_ref, out_ref, hb_scr, acc_scr):
    j = pl.program_id(1)

    @pl.when(j == 0)
    def _():
        hb_scr[...] = h_ref[...].astype(BF16)
        acc_scr[...] = jnp.zeros_like(acc_scr)

    hb = hb_scr[...]
    g = jnp.dot(hb, wg_ref[...], preferred_element_type=F32)
    u = jnp.dot(hb, wu_ref[...], preferred_element_type=F32)
    act = (_silu(g) * u).astype(BF16)
    acc_scr[...] += jnp.dot(act, wd_ref[...], preferred_element_type=F32)

    @pl.when(j == pl.num_programs(1) - 1)
    def _():
        out_ref[...] = _layer_norm(DN_ALPHA * h_ref[...] + acc_scr[...], ln_ref[0:1, :], ln_ref[1:2, :])


def _ffn_call(h2, wg, wu, wd, ln):
    t, d = h2.shape
    f = wg.shape[1]
    tm, fb = TM_FFN, FB_FFN
    return pl.pallas_call(
        _ffn_kernel,
        grid=(t // tm, f // fb),
        in_specs=[
            pl.BlockSpec((tm, d), lambda i, j: (i, 0)),
            pl.BlockSpec((d, fb), lambda i, j: (0, j)),
            pl.BlockSpec((d, fb), lambda i, j: (0, j)),
            pl.BlockSpec((fb, d), lambda i, j: (j, 0)),
            pl.BlockSpec(ln.shape, lambda i, j: (0, 0)),
        ],
        out_specs=pl.BlockSpec((tm, d), lambda i, j: (i, 0)),
        out_shape=jax.ShapeDtypeStruct((t, d), F32),
        scratch_shapes=[pltpu.VMEM((tm, d), BF16), pltpu.VMEM((tm, d), F32)],
        compiler_params=pltpu.CompilerParams(
            dimension_semantics=("arbitrary", "arbitrary"), vmem_limit_bytes=VMEM_LIMIT),
        name="ffn_dense",
    )(h2, wg, wu, wd, ln)


def _moe_kernel(h_ref, rt_ref, tri_ref, wg_ref, wu_ref, wd_ref, ln_ref, out_ref,
                hb_scr, col_scr, row_scr, acc_scr):
    e = pl.program_id(1)
    nsub = NSUB_MOE
    sub = h_ref.shape[0] // nsub
    csub = CSUB_MOE
    ne = N_EXPERTS

    @pl.when(e == 0)
    def _():
        h = h_ref[...]
        hb_scr[...] = h.astype(BF16)
        acc_scr[...] = jnp.zeros_like(acc_scr)
        logits = _dot_hl(h, rt_ref)
        lane = lax.broadcasted_iota(jnp.int32, logits.shape, 1)
        neg = jnp.float32(-jnp.inf)
        logits = jnp.where(lane < ne, logits, neg)
        m1 = jnp.max(logits, axis=-1, keepdims=True)
        i1 = jnp.min(jnp.where(logits == m1, lane, 128), axis=-1, keepdims=True)
        rest = jnp.where(lane == i1, neg, logits)
        m2 = jnp.max(rest, axis=-1, keepdims=True)
        i2 = jnp.min(jnp.where(rest == m2, lane, 128), axis=-1, keepdims=True)
        e2 = jnp.exp(m2 - m1)
        g1 = 1.0 / (1.0 + e2)
        g2 = e2 / (1.0 + e2)
        sel = (lane == i1) | (lane == i2)
        gates = jnp.where(lane == i1, g1, 0.0) + jnp.where(lane == i2, g2, 0.0)
        selb = jnp.where(sel, 1.0, 0.0).astype(BF16)
        tri = tri_ref[...]
        rank = jnp.concatenate(
            [jnp.dot(tri, selb[j * sub:(j + 1) * sub], preferred_element_type=F32) for j in range(nsub)],
            axis=0)
        slot = jnp.where(sel, rank, -1.0)
        packed = jnp.where(lane < ne, gates, pltpu.roll(slot, ne, 1))
        col_scr[...] = packed
        row_scr[...] = packed.T

    slot_row = row_scr[pl.ds(ne + e, 1), :]
    hit = jnp.where(slot_row >= 0.0, 1.0, 0.0)
    n_max = jnp.sum(hit[:, 0:sub])
    for j in range(1, nsub):
        n_max = jnp.maximum(n_max, jnp.sum(hit[:, j * sub:(j + 1) * sub]))
    n_tiles = (n_max.astype(jnp.int32) + (csub - 1)) // csub
    col = col_scr[...]
    lane = lax.broadcasted_iota(jnp.int32, col.shape, 1)
    slot_col = jnp.sum(jnp.where(lane == ne + e, col, 0.0), axis=-1, keepdims=True)
    gate_col = jnp.sum(jnp.where(lane == e, col, 0.0), axis=-1, keepdims=True)

    def tile(s, carry):
        base = (s * csub).astype(F32)
        jr = lax.broadcasted_iota(jnp.int32, (csub, sub), 0).astype(F32) + base
        xs = []
        for j in range(nsub):
            gather = jnp.where(jr == slot_row[:, j * sub:(j + 1) * sub], 1.0, 0.0).astype(BF16)
            xs.append(jnp.dot(gather, hb_scr[j * sub:(j + 1) * sub, :],
                              preferred_element_type=F32).astype(BF16))
        x = jnp.concatenate(xs, axis=0)
        g = jnp.dot(x, wg_ref[...], preferred_element_type=F32)
        u = jnp.dot(x, wu_ref[...], preferred_element_type=F32)
        act = (_silu(g) * u).astype(BF16)
        y = jnp.dot(act, wd_ref[...], preferred_element_type=F32).astype(BF16)
        jc = lax.broadcasted_iota(jnp.int32, (sub, csub), 1).astype(F32) + base
        for j in range(nsub):
            rows = slice(j * sub, (j + 1) * sub)
            scatter = jnp.where(jc == slot_col[rows], gate_col[rows], 0.0).astype(BF16)
            acc_scr[rows, :] += jnp.dot(scatter, y[j * csub:(j + 1) * csub],
                                        preferred_element_type=F32)
        return carry

    lax.fori_loop(0, n_tiles, tile, 0)

    @pl.when(e == pl.num_programs(1) - 1)
    def _():
        out_ref[...] = _layer_norm(DN_ALPHA * h_ref[...] + acc_scr[...], ln_ref[0:1, :], ln_ref[1:2, :])


def _moe_call(h2, router_hl, wg, wu, wd, ln):
    t, d = h2.shape
    ne, _, f = wg.shape
    tm = TM_MOE
    sub = tm // NSUB_MOE
    ti = jnp.arange(sub)
    tri = (ti[:, None] > ti[None, :]).astype(BF16)
    return pl.pallas_call(
        _moe_kernel,
        grid=(t // tm, ne),
        in_specs=[
            pl.BlockSpec((tm, d), lambda i, e: (i, 0)),
            pl.BlockSpec(router_hl.shape, lambda i, e: (0, 0, 0)),
            pl.BlockSpec((sub, sub), lambda i, e: (0, 0)),
            pl.BlockSpec((None, d, f), lambda i, e: (e, 0, 0)),
            pl.BlockSpec((None, d, f), lambda i, e: (e, 0, 0)),
            pl.BlockSpec((None, f, d), lambda i, e: (e, 0, 0)),
            pl.BlockSpec(ln.shape, lambda i, e: (0, 0)),
        ],
        out_specs=pl.BlockSpec((tm, d), lambda i, e: (i, 0)),
        out_shape=jax.ShapeDtypeStruct((t, d), F32),
        scratch_shapes=[pltpu.VMEM((tm, d), BF16), pltpu.VMEM((tm, 128), F32),
                        pltpu.VMEM((128, tm), F32), pltpu.VMEM((tm, d), F32)],
        compiler_params=pltpu.CompilerParams(
            dimension_semantics=("arbitrary", "arbitrary"), vmem_limit_bytes=VMEM_LIMIT),
        name="ffn_experts",
    )(h2, router_hl, tri, wg, wu, wd, ln)


def _trig_kernel(ang_ref, cos_ref, sin_ref):
    ang = ang_ref[...]
    cos_ref[...] = jnp.cos(ang)
    sin_ref[...] = jnp.sin(ang)


def _rotary_tables(positions):
    b, s = positions.shape
    half = HD // 2
    inv_freq = ROPE_BASE ** (-jnp.arange(half, dtype=F32) / half)
    ang = positions.astype(F32)[:, :, None] * inv_freq
    rows = b * s * half // 128
    ang2 = ang.reshape(rows, 128)
    blk = min(rows, 1024)
    cos2, sin2 = pl.pallas_call(
        _trig_kernel,
        grid=(rows // blk,),
        in_specs=[pl.BlockSpec((blk, 128), lambda i: (i, 0))],
        out_specs=(pl.BlockSpec((blk, 128), lambda i: (i, 0)),) * 2,
        out_shape=(jax.ShapeDtypeStruct((rows, 128), F32),) * 2,
        name="rotary_tables",
    )(ang2)
    cos = cos2.reshape(b, s, half)
    sin = sin2.reshape(b, s, half)
    cos_t = jnp.concatenate([cos, cos, cos, cos], axis=-1)
    sin_t = jnp.concatenate([-sin, sin, -sin, sin], axis=-1)
    return cos_t, sin_t


def _hl(w):
    hi = w.astype(BF16)
    lo = (w - hi.astype(F32)).astype(BF16)
    return jnp.stack([hi, lo])


def _pad_to(a, shape):
    return jnp.pad(a, [(0, t - s) for s, t in zip(a.shape, shape)])


def _pack_w_in(w):
    d = w.shape[0]
    z = lambda n: jnp.zeros((d, n), w.dtype)
    gl, rw, rt = 1024, 1808, 2864
    cols = [
        w[:, 0:1024],
        w[:, gl:gl + 512], w[:, gl + 528:gl + 784], w[:, gl + 512:gl + 528], z(112),
        w[:, rw:rw + 1056], z(96),
        w[:, rt:rt + 1024],
    ]
    return jnp.concatenate(cols, axis=1).astype(BF16)


def kernel(x, positions, w_in, w_out, ln1_g, ln1_b, ln2_g, ln2_b, hgrn_lb_logits, hgrn_norm_g,
           gla_gate_w2, gla_gate_b, gla_norm_g, rwkv_mu, rwkv_w0, rwkv_w2, rwkv_a0, rwkv_a2,
           rwkv_g2, rwkv_k_k, rwkv_k_a, rwkv_r_k, rwkv_lnx_g, rwkv_lnx_b, rwkv_v0, rwkv_v1,
           rwkv_v2, ffn_w_gate, ffn_w_up, ffn_w_down, moe_router, moe_w_gate, moe_w_up,
           moe_w_down):
    out_dtype = x.dtype
    b, s, d = x.shape
    h = x.astype(F32)
    cos_t, sin_t = _rotary_tables(positions)

    lb_all = jnp.cumsum(jax.nn.softmax(hgrn_lb_logits.astype(F32), axis=0), axis=0)
    lb_all = lb_all - lb_all[:1]
    log_gamma = jnp.log(1.0 - 2.0 ** (-5.0 - jnp.arange(NH, dtype=F32)))
    log_gamma_row = jnp.repeat(log_gamma, HD)
    head_i = jnp.arange(GROUP) // HD
    hs = (head_i[:, None] == head_i[None, :]).astype(BF16)
    tile4 = lambda v: jnp.tile(v, NH)
    cum = (jnp.arange(CH, dtype=F32)[:, None] + 1.0) * log_gamma_row[None, :]
    mid, cend = cum[CH // 2:CH // 2 + 1], cum[CH - 1:CH]
    rfac = jnp.stack([jnp.exp(cum), jnp.exp(cum - mid), jnp.exp(mid - cum), jnp.exp(cend - cum),
                      jnp.broadcast_to(jnp.exp(cend), cum.shape)])
    zrow = jnp.zeros((GROUP,), F32)

    v_first = None
    for l in range(DEPTH):
        rows = [
            lb_all[l], tile4(hgrn_norm_g[l]), _pad_to(gla_gate_b[l], (GROUP,)), tile4(gla_norm_g[l]),
            rwkv_w0[l], rwkv_a0[l], rwkv_k_k[l], rwkv_k_a[l], rwkv_r_k[l].reshape(GROUP),
            rwkv_lnx_g[l], rwkv_lnx_b[l], rwkv_v0[l - 1] if l > 0 else zrow, log_gamma_row,
            zrow, zrow, zrow,
        ]
        vecs = jnp.stack(rows).astype(F32)
        mu = _pad_to(rwkv_mu[l][None, :], (8, RWW))
        ln1 = _pad_to(jnp.stack([ln1_g[l], ln1_b[l]]), (8, d))
        ln2 = _pad_to(jnp.stack([ln2_g[l], ln2_b[l]]), (8, d))
        glaw2 = _hl(_pad_to(gla_gate_w2[l], (128, 128)))
        w2a2 = jnp.zeros((128, 512), F32)
        w2a2 = w2a2.at[0:64, 0:256].set(rwkv_w2[l]).at[64:128, 256:512].set(rwkv_a2[l])
        w2a2 = _hl(w2a2)
        g2 = _pad_to(rwkv_g2[l], (256, 256)).astype(BF16)
        win = _pack_w_in(w_in[l])
        wout = w_out[l].astype(BF16)
        if l == 0:
            h, v_first = _mixer_call(True, h, cos_t, sin_t, win, wout, vecs, mu, ln1, glaw2, w2a2,
                                     g2, hs, rfac)
        else:
            v1 = _pad_to(rwkv_v1[l - 1], (256, 128)).astype(BF16)
            v2 = _pad_to(rwkv_v2[l - 1], (128, 256)).astype(BF16)
            h = _mixer_call(False, h, cos_t, sin_t, win, wout, vecs, mu, ln1, glaw2, w2a2, g2, hs,
                            rfac, v_first, v1, v2)
        h2 = h.reshape(b * s, d)
        i = l // 2
        if l % 2 == 0:
            h2 = _ffn_call(h2, ffn_w_gate[i].astype(BF16), ffn_w_up[i].astype(BF16),
                           ffn_w_down[i].astype(BF16), ln2)
        else:
            router = _hl(_pad_to(moe_router[i], (d, 128)))
            h2 = _moe_call(h2, router, moe_w_gate[i].astype(BF16), moe_w_up[i].astype(BF16),
                           moe_w_down[i].astype(BF16), ln2)
        h = h2.reshape(b, s, d)
    return h.astype(out_dtype)
```

```python
import functools

import jax
import jax.numpy as jnp
from jax import lax
from jax.experimental import pallas as pl
from jax.experimental.pallas import tpu as pltpu

F32 = jnp.float32
BF16 = jnp.bfloat16

D_MODEL = 1024
DEPTH = 4
GROUP = 256
NH = 4
HD = 64
CH = HD
GLA_KEY = 32
GLA_GATE_NORM = 16.0
RWKV_LNX_EPS = 64e-5
ROPE_BASE = 10000.0
FFN_DENSE = 2816
N_EXPERTS = 8
FFN_EXPERT = 1408
DN_ALPHA = (2.0 * DEPTH) ** 0.25
LN_EPS = 1e-5

HG0 = 0
GL0 = 1024
RW0 = 1920
RWW = 1152
RT0 = 3072
IN_PAD = 4096

TB_MIX = 128
TM_FFN = 512
FB_FFN = 256
TM_MOE = 1024
NSUB_MOE = 2
CSUB_MOE = 160
VMEM_LIMIT = 56 * 1024 * 1024


def _dot(a, b):
    return jnp.dot(a.astype(BF16), b.astype(BF16), preferred_element_type=F32)


def _dot_nt(a, b):
    return lax.dot_general(a.astype(BF16), b.astype(BF16), (((1,), (1,)), ((), ())),
                           preferred_element_type=F32)


def _dot_tn(a, b):
    return lax.dot_general(a.astype(BF16), b.astype(BF16), (((0,), (0,)), ((), ())),
                           preferred_element_type=F32)


def _split2(x):
    hi = x.astype(BF16)
    lo = (x - hi.astype(F32)).astype(BF16)
    return hi, lo


def _dot_hl(a, w_ref):
    a_hi, a_lo = _split2(a)
    w_hi = w_ref[0]
    w_lo = w_ref[1]
    return (jnp.dot(a_hi, w_hi, preferred_element_type=F32)
            + jnp.dot(a_lo, w_hi, preferred_element_type=F32)
            + jnp.dot(a_hi, w_lo, preferred_element_type=F32))


def _sigmoid(x):
    return 1.0 / (1.0 + jnp.exp(-x))


def _silu(x):
    return x * _sigmoid(x)


def _softplus(x):
    return jnp.maximum(x, 0.0) + jnp.log(1.0 + jnp.exp(-jnp.abs(x)))


def _layer_norm(x, g, b):
    mu = jnp.mean(x, axis=-1, keepdims=True)
    xc = x - mu
    var = jnp.mean(xc * xc, axis=-1, keepdims=True)
    return xc * lax.rsqrt(var + LN_EPS) * g + b


def _cumsum_rows(x, tri):
    hi, lo = _split2(x)
    return jnp.dot(tri, hi, preferred_element_type=F32) + jnp.dot(tri, lo, preferred_element_type=F32)


def _head_mean(x, hs):
    return jnp.dot(x.astype(BF16), hs, preferred_element_type=F32) * (1.0 / HD)


def _stack_heads(x, group):
    lane = lax.broadcasted_iota(jnp.int32, (1, x.shape[1]), 1)
    parts = [jnp.where((lane // group) == h, x, 0.0) for h in range(NH)]
    return jnp.concatenate(parts, axis=0)


def _run_interleaved(gens):
    alive = list(gens)
    while alive:
        still = []
        for g in alive:
            try:
                next(g)
                still.append(g)
            except StopIteration:
                pass
        alive = still


def _lockstep(gens):
    results = [None] * len(gens)
    alive = list(range(len(gens)))
    while alive:
        still = []
        for i in alive:
            try:
                next(gens[i])
                still.append(i)
            except StopIteration as stop:
                results[i] = stop.value
        alive = still
        if alive:
            yield
    return results


def _gla_core(q, k, v, st_ref, kgroup, tri, logf=None, factors=None):
    c = q.shape[0]
    kt = q.shape[1]
    if factors is None:
        cum = _cumsum_rows(logf, tri)
        yield
        mid = cum[c // 2:c // 2 + 1]
        cend = cum[c - 1:c]
        e_i = jnp.exp(cum)
        e_a = jnp.exp(cum - mid)
        e_k = jnp.exp(mid - cum)
        e_s = jnp.exp(cend - cum)
        e_end = jnp.exp(cend)
    else:
        e_i, e_a, e_k, e_s, e_end = factors
    a = _dot_nt(q * e_a, _stack_heads(k * e_k, kgroup))
    st = st_ref[...]
    o_int = _dot_nt(q * e_i, st)
    upd = _dot_tn(v, k * e_s)
    yield
    row = lax.broadcasted_iota(jnp.int32, a.shape, 0)
    col = lax.broadcasted_iota(jnp.int32, a.shape, 1) & (c - 1)
    a = jnp.where(row >= col, a, 0.0).astype(BF16)
    r_h = lax.broadcasted_iota(jnp.int32, (NH * HD, kt), 0) // HD
    c_h = lax.broadcasted_iota(jnp.int32, (NH * HD, kt), 1) // kgroup
    st_ref[...] = st * e_end + jnp.where(r_h == c_h, upd, 0.0)
    o_intra = jnp.dot(a, _stack_heads(v, HD).astype(BF16), preferred_element_type=F32)
    yield
    return o_int + o_intra


def _rwkv_core(r, logw, k, v, kk, a, st_ref, tri):
    c = r.shape[0]
    n = NH * c
    cum = _cumsum_rows(logw, tri)
    yield
    cump = cum - logw
    mid = cum[c // 2:c // 2 + 1]
    cend = cum[c - 1:c]
    eq = jnp.exp(cum - mid)
    eqp = jnp.exp(cump - mid)
    ek = jnp.exp(mid - cum)
    es = jnp.exp(cend - cum)
    ei = jnp.exp(cum)
    eip = jnp.exp(cump)
    b = kk * a
    na = -kk

    def stack(x):
        return _stack_heads(x, HD).astype(BF16)

    lhs = jnp.concatenate([na * eqp, r * eq], axis=0).astype(BF16)
    rhs = jnp.concatenate([stack(b * ek), stack(k * ek)], axis=0)
    sc = lax.dot_general(lhs, rhs, (((1,), (1,)), ((), ())), preferred_element_type=F32)
    st = st_ref[...]
    stb = st.astype(BF16)
    rs = _dot_nt(jnp.concatenate([na * eip, r * ei], axis=0), stb)
    rs_u = rs[:c]
    rs_o = rs[c:]
    vs = stack(v)
    yield
    i0 = lax.broadcasted_iota(jnp.int32, (c, n), 0)
    i1 = lax.broadcasted_iota(jnp.int32, (c, n), 1) & (c - 1)
    strict = i0 > i1
    incl = i0 >= i1
    a_ab = jnp.where(strict, sc[:c, :n], 0.0)
    a_ak = jnp.where(strict, sc[:c, n:], 0.0).astype(BF16)
    a_rb = jnp.where(incl, sc[c:, :n], 0.0).astype(BF16)
    a_rk = jnp.where(incl, sc[c:, n:], 0.0).astype(BF16)
    same16 = (i0 >> 4) == (i1 >> 4)
    eye = jnp.where(i0 == i1, 1.0, 0.0).astype(F32)
    dm = jnp.where(same16, a_ab, 0.0)
    em = a_ab - dm
    p = eye + dm
    d2 = _dot(dm, stack(dm))
    av = jnp.dot(jnp.concatenate([a_ak, a_rk], axis=0), vs, preferred_element_type=F32)
    ak_v = av[:c]
    rk_v = av[c:]
    yield
    pd = _dot(jnp.concatenate([p, d2], axis=0), stack(d2))
    yield
    p = p + pd[:c]
    d4 = pd[c:]
    pd = _dot(jnp.concatenate([p, d4], axis=0), stack(d4))
    yield
    p = p + pd[:c]
    d8 = pd[c:]
    pd = _dot(p, stack(d8))
    yield
    p = p + pd
    f = _dot(p, stack(em))
    yield
    f2 = _dot(f, stack(f))
    yield
    ff2 = _dot(f, stack(f2))
    yield
    g = eye + f + f2 + ff2
    t = _dot(g, stack(p))
    yield
    u = _dot(t, stack(rs_u + ak_v))
    yield
    rb_u = jnp.dot(a_rb, stack(u), preferred_element_type=F32)
    s_new = _dot_tn(jnp.concatenate([u, v], axis=0), jnp.concatenate([b * es, k * es], axis=0))
    yield
    r_h = lax.broadcasted_iota(jnp.int32, s_new.shape, 0) // HD
    c_h = lax.broadcasted_iota(jnp.int32, s_new.shape, 1) // HD
    st_ref[...] = st * jnp.exp(cend) + jnp.where(r_h == c_h, s_new, 0.0)
    return rs_o + rb_u + rk_v


def _mixer_kernel(first, nb, tb, *refs):
    if first:
        (h_ref, cos_ref, sin_ref, win_ref, wout_ref, vec_ref, mu_ref, ln_ref, glaw2_ref, w2a2_ref,
         g2_ref, hs_ref, rfac_ref, out_ref, vf_out_ref, p_scr, y_scr, st_hg, st_gl, st_rw,
         st_rt) = refs
        vf_in_ref = v1_ref = v2_ref = None
    else:
        (h_ref, cos_ref, sin_ref, win_ref, wout_ref, vec_ref, mu_ref, ln_ref, glaw2_ref, w2a2_ref,
         g2_ref, hs_ref, rfac_ref, vf_in_ref, v1_ref, v2_ref, out_ref, p_scr, y_scr, st_hg, st_gl,
         st_rw, st_rt) = refs
        vf_out_ref = None
    c = CH
    t_idx = pl.program_id(0)

    @pl.when(t_idx == 0)
    def _():
        st_hg[...] = jnp.zeros_like(st_hg)
        st_gl[...] = jnp.zeros_like(st_gl)
        st_rw[...] = jnp.zeros_like(st_rw)
        st_rt[...] = jnp.zeros_like(st_rt)
        p_scr[:, 0:8, RW0:RW0 + RWW] = jnp.zeros((nb, 8, RWW), F32)

    @pl.when(t_idx > 0)
    def _():
        p_scr[:, 0:8, RW0:RW0 + RWW] = p_scr[:, tb:tb + 8, RW0:RW0 + RWW]

    hb = h_ref[...].reshape(nb * tb, D_MODEL).astype(BF16)
    for j in range(IN_PAD // 1024):
        res = jnp.dot(hb, win_ref[:, j * 1024:(j + 1) * 1024], preferred_element_type=F32)
        for bi in range(nb):
            p_scr[bi, 8:tb + 8, j * 1024:(j + 1) * 1024] = res[bi * tb:(bi + 1) * tb]

    def vec(i, n=GROUP):
        return vec_ref[i:i + 1, 0:n]

    def chunk_body(ci, carry):
        r0 = pl.multiple_of(ci * c, c)
        rows = pl.ds(r0 + 8, c)
        crow = pl.ds(r0, c)
        ti = lax.broadcasted_iota(jnp.int32, (c, c), 0)
        tj = lax.broadcasted_iota(jnp.int32, (c, c), 1)
        tri = jnp.where(ti >= tj, 1.0, 0.0).astype(BF16)

        def ld(c0, c1):
            return jnp.concatenate([p_scr[bi, rows, c0:c1] for bi in range(nb)], axis=0)

        def sl(x, bi):
            return x[bi * c:(bi + 1) * c]

        def store_y(c0, val):
            for bi in range(nb):
                y_scr[pl.ds(bi * tb + r0, c), c0:c0 + GROUP] = sl(val, bi).astype(BF16)

        def hgrn():
            q = _silu(ld(HG0, HG0 + 256))
            lb = vec(0)
            fg = lb + (1.0 - lb) * _sigmoid(ld(HG0 + 256, HG0 + 512))
            k = 1.0 - fg
            logf = jnp.log(fg)
            v = ld(HG0 + 512, HG0 + 768)
            outs = yield from _lockstep([
                _gla_core(sl(q, bi), sl(k, bi), sl(v, bi), st_hg.at[bi], HD, tri, logf=sl(logf, bi))
                for bi in range(nb)])
            o = jnp.concatenate(outs, axis=0)
            ms = _head_mean(o * o, hs_ref[...])
            yield
            o = o * lax.rsqrt(ms + 1e-6) * vec(1) * _silu(ld(HG0 + 768, HG0 + 1024))
            store_y(0, o)

        def gla():
            z = _dot_hl(ld(GL0 + 768, GL0 + 896), glaw2_ref)
            yield
            logf = -_softplus(-(z + vec(2, 128))) * (1.0 / GLA_GATE_NORM)
            q = ld(GL0, GL0 + 128) * (GLA_KEY ** -0.5)
            k = ld(GL0 + 128, GL0 + 256)
            v = ld(GL0 + 256, GL0 + 512)
            outs = yield from _lockstep([
                _gla_core(sl(q, bi), sl(k, bi), sl(v, bi), st_gl.at[bi], GLA_KEY, tri,
                          logf=sl(logf, bi))
                for bi in range(nb)])
            o = jnp.concatenate(outs, axis=0)
            ms = _head_mean(o * o, hs_ref[...])
            yield
            o = o * lax.rsqrt(ms + 1e-6) * vec(3) * _silu(ld(GL0 + 512, GL0 + 768))
            store_y(256, o)

        def rwkv():
            hs = hs_ref[...]
            curs, prevs = [], []
            for bi in range(nb):
                xs = p_scr[bi, pl.ds(r0, c + 8), RW0:RW0 + RWW]
                curs.append(xs[8:])
                prevs.append(pltpu.roll(xs, 1, 0)[8:])
            cur = jnp.concatenate(curs, axis=0)
            prev = jnp.concatenate(prevs, axis=0)
            pm = cur + (prev - cur) * mu_ref[0:1, :]
            r = pm[:, 0:256]
            k = pm[:, 256:512]
            v = pm[:, 512:768]
            wa = pm[:, 768:896]
            lane = lax.broadcasted_iota(jnp.int32, (1, 128), 1)
            pre = _dot_hl(jnp.where(lane < 64, jnp.tanh(wa), wa), w2a2_ref)
            gate = _dot(_sigmoid(pm[:, 896:1152]), g2_ref[...])
            kk = k * vec(6)
            kk_ms = _head_mean(kk * kk, hs)
            if first:
                for bi in range(nb):
                    vf_out_ref[bi, crow, :] = sl(v, bi)
            else:
                t1 = _dot(v, v1_ref[...])
            yield
            w = -_softplus(-(vec(4) + pre[:, 0:256])) - 0.5
            logw = -jnp.exp(w)
            a = _sigmoid(vec(5) + pre[:, 256:512])
            kk = kk / jnp.maximum(jnp.sqrt(kk_ms * HD), 1e-12)
            k = k * (1.0 + (a - 1.0) * vec(7))
            rk = _head_mean(r * k * vec(8), hs)
            if not first:
                t2 = _dot(t1, v2_ref[...])
                yield
                vf = jnp.concatenate([vf_in_ref[bi, crow, :] for bi in range(nb)], axis=0)
                v = v + (vf - v) * _sigmoid(vec(11) + t2)
            outs = yield from _lockstep([
                _rwkv_core(sl(r, bi), sl(logw, bi), sl(k, bi), sl(v, bi), sl(kk, bi), sl(a, bi),
                           st_rw.at[bi], tri)
                for bi in range(nb)])
            o = jnp.concatenate(outs, axis=0)
            mu_o = _head_mean(o, hs)
            yield
            oc = o - mu_o
            var = _head_mean(oc * oc, hs)
            yield
            o = oc * lax.rsqrt(var + RWKV_LNX_EPS) * vec(9) + vec(10)
            o = o + (rk * HD) * v
            store_y(512, o * gate)

        def retnet():
            hs = hs_ref[...]
            q = ld(RT0, RT0 + 256)
            k = ld(RT0 + 256, RT0 + 512)
            v = ld(RT0 + 512, RT0 + 768)
            cs = jnp.concatenate([cos_ref[bi, crow, :] for bi in range(nb)], axis=0)
            sn = jnp.concatenate([sin_ref[bi, crow, :] for bi in range(nb)], axis=0)
            cs = jnp.concatenate([cs, cs], axis=1)
            sn = jnp.concatenate([sn, sn], axis=1)
            lane_r = lax.broadcasted_iota(jnp.int32, (1, 256), 1)
            first_half = (lane_r & (HD - 1)) < (HD // 2)

            def rot(x):
                swapped = jnp.where(first_half, pltpu.roll(x, 256 - HD // 2, 1),
                                    pltpu.roll(x, HD // 2, 1))
                return x * cs + swapped * sn

            q = rot(q)
            k = rot(k) * (HD ** -0.5)
            factors = (rfac_ref[0], rfac_ref[1], rfac_ref[2], rfac_ref[3], rfac_ref[4, 0:1, :])
            outs = yield from _lockstep([
                _gla_core(sl(q, bi), sl(k, bi), sl(v, bi), st_rt.at[bi], HD, tri, factors=factors)
                for bi in range(nb)])
            o = jnp.concatenate(outs, axis=0)
            mu_o = _head_mean(o, hs)
            yield
            oc = o - mu_o
            var = _head_mean(oc * oc, hs)
            yield
            store_y(768, _silu(ld(RT0 + 768, RT0 + 1024)) * oc * lax.rsqrt(var + LN_EPS))

        _run_interleaved([rwkv(), hgrn(), gla(), retnet()])
        return carry

    lax.fori_loop(0, tb // c, chunk_body, 0)

    mix = jnp.dot(y_scr[...], wout_ref[...], preferred_element_type=F32)
    hres = DN_ALPHA * h_ref[...].reshape(nb * tb, D_MODEL) + mix
    out_ref[...] = _layer_norm(hres, ln_ref[0:1, :], ln_ref[1:2, :]).reshape(nb, tb, D_MODEL)


def _mixer_call(layer, h, cos_t, sin_t, w_in, w_out, vecs, mu, ln, glaw2, w2a2, g2, hs, rfac,
                vf=None, v1=None, v2=None):
    first = layer == 0
    b, s, d = h.shape
    tb = TB_MIX
    nb = b
    grid = (s // tb,)

    def tok(width):
        return pl.BlockSpec((nb, tb, width), lambda j: (0, j, 0))

    def full(arr):
        nd = arr.ndim
        return pl.BlockSpec(arr.shape, lambda j: (0,) * nd)

    def of_layer(arr, li):
        nd = arr.ndim
        return pl.BlockSpec((None,) + arr.shape[1:], lambda j: (li,) + (0,) * (nd - 1))

    per_layer = [w_in, w_out, vecs, mu, ln, glaw2, w2a2, g2]
    args = [h, cos_t, sin_t] + per_layer + [hs, rfac]
    in_specs = ([tok(d), tok(128), tok(128)] + [of_layer(a, layer) for a in per_layer]
                + [full(hs), full(rfac)])
    out_h = jax.ShapeDtypeStruct((b, s, d), F32)
    if first:
        out_shape = (out_h, jax.ShapeDtypeStruct((b, s, GROUP), F32))
        out_specs = (tok(d), tok(GROUP))
    else:
        args += [vf, v1, v2]
        in_specs += [tok(GROUP), of_layer(v1, layer - 1), of_layer(v2, layer - 1)]
        out_shape = out_h
        out_specs = tok(d)
    scratch = [
        pltpu.VMEM((nb, tb + 8, IN_PAD), F32),
        pltpu.VMEM((nb * tb, d), BF16),
        pltpu.VMEM((nb, GROUP, GROUP), F32),
        pltpu.VMEM((nb, GROUP, NH * GLA_KEY), F32),
        pltpu.VMEM((nb, GROUP, GROUP), F32),
        pltpu.VMEM((nb, GROUP, GROUP), F32),
    ]
    return pl.pallas_call(
        functools.partial(_mixer_kernel, first, nb, tb),
        grid=grid,
        in_specs=in_specs,
        out_specs=out_specs,
        out_shape=out_shape,
        scratch_shapes=scratch,
        compiler_params=pltpu.CompilerParams(
            dimension_semantics=("arbitrary",), vmem_limit_bytes=VMEM_LIMIT),
        name="token_mixer_first" if first else "token_mixer",
    )(*args)


def _ffn_kernel(h_ref, wg_ref, wu_ref, wd_ref, ln_ref, out_ref, acc_scr):
    hb = h_ref[...].astype(BF16)
    fb = FB_FFN
    for k in range(wg_ref.shape[1] // fb):
        cols = slice(k * fb, (k + 1) * fb)
        g = jnp.dot(hb, wg_ref[:, cols], preferred_element_type=F32)
        u = jnp.dot(hb, wu_ref[:, cols], preferred_element_type=F32)
        act = (_silu(g) * u).astype(BF16)
        y = jnp.dot(act, wd_ref[cols, :], preferred_element_type=F32)
        if k == 0:
            acc_scr[...] = y
        else:
            acc_scr[...] += y
    out_ref[...] = _layer_norm(DN_ALPHA * h_ref[...] + acc_scr[...], ln_ref[0:1, :], ln_ref[1:2, :])


def _ffn_call(h2, wg, wu, wd, ln, li, layer):
    t, d = h2.shape
    f = wg.shape[2]
    tm = TM_FFN
    return pl.pallas_call(
        _ffn_kernel,
        grid=(t // tm,),
        in_specs=[
            pl.BlockSpec((tm, d), lambda i: (i, 0)),
            pl.BlockSpec((None, d, f), lambda i: (li, 0, 0)),
            pl.BlockSpec((None, d, f), lambda i: (li, 0, 0)),
            pl.BlockSpec((None, f, d), lambda i: (li, 0, 0)),
            pl.BlockSpec((None,) + ln.shape[1:], lambda i: (layer, 0, 0)),
        ],
        out_specs=pl.BlockSpec((tm, d), lambda i: (i, 0)),
        out_shape=jax.ShapeDtypeStruct((t, d), F32),
        scratch_shapes=[pltpu.VMEM((tm, d), F32)],
        compiler_params=pltpu.CompilerParams(
            dimension_semantics=("arbitrary",), vmem_limit_bytes=VMEM_LIMIT),
        name="ffn_dense",
    )(h2, wg, wu, wd, ln)


def _moe_kernel(h_ref, rt_ref, tri_ref, wg_ref, wu_ref, wd_ref, ln_ref, out_ref,
                hb_scr, row_scr, acc_scr):
    e = pl.program_id(1)
    nsub = NSUB_MOE
    sub = h_ref.shape[0] // nsub
    csub = CSUB_MOE
    ne = N_EXPERTS

    @pl.when(e == 0)
    def _():
        h = h_ref[...]
        hb_scr[...] = h.astype(BF16)
        acc_scr[...] = jnp.zeros_like(acc_scr)
        logits = _dot_hl(h, rt_ref)
        lane = lax.broadcasted_iota(jnp.int32, logits.shape, 1)
        neg = jnp.float32(-jnp.inf)
        logits = jnp.where(lane < ne, logits, neg)
        m1 = jnp.max(logits, axis=-1, keepdims=True)
        i1 = jnp.min(jnp.where(logits == m1, lane, 128), axis=-1, keepdims=True)
        rest = jnp.where(lane == i1, neg, logits)
        m2 = jnp.max(rest, axis=-1, keepdims=True)
        i2 = jnp.min(jnp.where(rest == m2, lane, 128), axis=-1, keepdims=True)
        e2 = jnp.exp(m2 - m1)
        g1 = 1.0 / (1.0 + e2)
        g2 = e2 / (1.0 + e2)
        sel = (lane == i1) | (lane == i2)
        gates = jnp.where(lane == i1, g1, 0.0) + jnp.where(lane == i2, g2, 0.0)
        selb = jnp.where(sel, 1.0, 0.0).astype(BF16)
        tri = tri_ref[...]
        rank = jnp.concatenate(
            [jnp.dot(tri, selb[j * sub:(j + 1) * sub], preferred_element_type=F32) for j in range(nsub)],
            axis=0)
        slot = jnp.where(sel, rank, -1.0)
        packed = jnp.where(lane < ne, gates, pltpu.roll(slot, ne, 1))
        row_scr[...] = packed.T

    slot_row = row_scr[pl.ds(ne + e, 1), :]
    hit = jnp.where(slot_row >= 0.0, 1.0, 0.0)
    n_max = jnp.sum(hit[:, 0:sub])
    for j in range(1, nsub):
        n_max = jnp.maximum(n_max, jnp.sum(hit[:, j * sub:(j + 1) * sub]))
    n_tiles = (n_max.astype(jnp.int32) + (csub - 1)) // csub
    gate_row = row_scr[pl.ds(e, 1), :]

    def tile(s, carry):
        base = (s * csub).astype(F32)
        jr = lax.broadcasted_iota(jnp.int32, (csub, sub), 0).astype(F32) + base
        xs, hits = [], []
        for j in range(nsub):
            hits.append(jr == slot_row[:, j * sub:(j + 1) * sub])
            xs.append(jnp.dot(jnp.where(hits[j], 1.0, 0.0).astype(BF16), hb_scr[j * sub:(j + 1) * sub, :],
                              preferred_element_type=F32).astype(BF16))
        x = jnp.concatenate(xs, axis=0)
        g = jnp.dot(x, wg_ref[...], preferred_element_type=F32)
        u = jnp.dot(x, wu_ref[...], preferred_element_type=F32)
        act = (_silu(g) * u).astype(BF16)
        y = jnp.dot(act, wd_ref[...], preferred_element_type=F32).astype(BF16)
        for j in range(nsub):
            rows = slice(j * sub, (j + 1) * sub)
            weighted = jnp.where(hits[j], gate_row[:, rows], 0.0)
            acc_scr[rows, :] += _dot_tn(weighted, y[j * csub:(j + 1) * csub])
        return carry

    lax.fori_loop(0, n_tiles, tile, 0)

    @pl.when(e == pl.num_programs(1) - 1)
    def _():
        out_ref[...] = _layer_norm(DN_ALPHA * h_ref[...] + acc_scr[...], ln_ref[0:1, :], ln_ref[1:2, :])


def _moe_call(h2, router_hl, wg, wu, wd, ln, li, layer):
    t, d = h2.shape
    _, ne, _, f = wg.shape
    tm = TM_MOE
    sub = tm // NSUB_MOE
    ti = jnp.arange(sub)
    tri = (ti[:, None] > ti[None, :]).astype(BF16)
    return pl.pallas_call(
        _moe_kernel,
        grid=(t // tm, ne),
        in_specs=[
            pl.BlockSpec((tm, d), lambda i, e: (i, 0)),
            pl.BlockSpec((None,) + router_hl.shape[1:], lambda i, e: (li, 0, 0, 0)),
            pl.BlockSpec((sub, sub), lambda i, e: (0, 0)),
            pl.BlockSpec((None, None, d, f), lambda i, e: (li, e, 0, 0)),
            pl.BlockSpec((None, None, d, f), lambda i, e: (li, e, 0, 0)),
            pl.BlockSpec((None, None, f, d), lambda i, e: (li, e, 0, 0)),
            pl.BlockSpec((None,) + ln.shape[1:], lambda i, e: (layer, 0, 0)),
        ],
        out_specs=pl.BlockSpec((tm, d), lambda i, e: (i, 0)),
        out_shape=jax.ShapeDtypeStruct((t, d), F32),
        scratch_shapes=[pltpu.VMEM((tm, d), BF16),
                        pltpu.VMEM((128, tm), F32), pltpu.VMEM((tm, d), F32)],
        compiler_params=pltpu.CompilerParams(
            dimension_semantics=("arbitrary", "arbitrary"), vmem_limit_bytes=VMEM_LIMIT),
        name="ffn_experts",
    )(h2, router_hl, tri, wg, wu, wd, ln)


def _trig_kernel(ang_ref, cos_ref, sin_ref):
    ang = ang_ref[...]
    cos_ref[...] = jnp.cos(ang)
    sin_ref[...] = jnp.sin(ang)


def _rotary_tables(positions):
    b, s = positions.shape
    half = HD // 2
    inv_freq = ROPE_BASE ** (-jnp.arange(half, dtype=F32) / half)
    ang = positions.astype(F32)[:, :, None] * inv_freq
    rows = b * s * half // 128
    ang2 = ang.reshape(rows, 128)
    blk = min(rows, 1024)
    cos2, sin2 = pl.pallas_call(
        _trig_kernel,
        grid=(rows // blk,),
        in_specs=[pl.BlockSpec((blk, 128), lambda i: (i, 0))],
        out_specs=(pl.BlockSpec((blk, 128), lambda i: (i, 0)),) * 2,
        out_shape=(jax.ShapeDtypeStruct((rows, 128), F32),) * 2,
        name="rotary_tables",
    )(ang2)
    cos = cos2.reshape(b, s, half)
    sin = sin2.reshape(b, s, half)
    cos_t = jnp.concatenate([cos, cos, cos, cos], axis=-1)
    sin_t = jnp.concatenate([-sin, sin, -sin, sin], axis=-1)
    return cos_t, sin_t


def _hl(w):
    hi = w.astype(BF16)
    lo = (w - hi.astype(F32)).astype(BF16)
    return jnp.stack([hi, lo], axis=1)


def _pad_to(a, shape):
    return jnp.pad(a, [(0, t - s) for s, t in zip(a.shape, shape)])


def _pack_w_in(w):
    w = w.astype(BF16)
    z = lambda n: jnp.zeros(w.shape[:2] + (n,), BF16)
    gl, rw, rt = 1024, 1808, 2864
    cols = [
        w[..., 0:1024],
        w[..., gl:gl + 512], w[..., gl + 528:gl + 784], w[..., gl + 512:gl + 528], z(112),
        w[..., rw:rw + 1056], z(96),
        w[..., rt:rt + 1024],
    ]
    return jnp.concatenate(cols, axis=-1)


def kernel(x, positions, w_in, w_out, ln1_g, ln1_b, ln2_g, ln2_b, hgrn_lb_logits, hgrn_norm_g,
           gla_gate_w2, gla_gate_b, gla_norm_g, rwkv_mu, rwkv_w0, rwkv_w2, rwkv_a0, rwkv_a2,
           rwkv_g2, rwkv_k_k, rwkv_k_a, rwkv_r_k, rwkv_lnx_g, rwkv_lnx_b, rwkv_v0, rwkv_v1,
           rwkv_v2, ffn_w_gate, ffn_w_up, ffn_w_down, moe_router, moe_w_gate, moe_w_up,
           moe_w_down):
    out_dtype = x.dtype
    b, s, d = x.shape
    h = x.astype(F32)
    cos_t, sin_t = _rotary_tables(positions)

    lb_all = jnp.cumsum(jax.nn.softmax(hgrn_lb_logits.astype(F32), axis=0), axis=0)
    lb_all = lb_all - lb_all[:1]
    log_gamma = jnp.log(1.0 - 2.0 ** (-5.0 - jnp.arange(NH, dtype=F32)))
    log_gamma_row = jnp.repeat(log_gamma, HD)
    head_i = jnp.arange(GROUP) // HD
    hs = (head_i[:, None] == head_i[None, :]).astype(BF16)
    cum = (jnp.arange(CH, dtype=F32)[:, None] + 1.0) * log_gamma_row[None, :]
    mid, cend = cum[CH // 2:CH // 2 + 1], cum[CH - 1:CH]
    rfac = jnp.stack([jnp.exp(cum), jnp.exp(cum - mid), jnp.exp(mid - cum), jnp.exp(cend - cum),
                      jnp.broadcast_to(jnp.exp(cend), cum.shape)])

    nl = DEPTH
    tile4 = lambda v: jnp.tile(v, (1, NH))
    zrows = jnp.zeros((nl, GROUP), F32)
    vecs = jnp.stack([
        lb_all, tile4(hgrn_norm_g), _pad_to(gla_gate_b, (nl, GROUP)), tile4(gla_norm_g),
        rwkv_w0, rwkv_a0, rwkv_k_k, rwkv_k_a, rwkv_r_k.reshape(nl, GROUP),
        rwkv_lnx_g, rwkv_lnx_b, jnp.concatenate([zrows[:1], rwkv_v0], axis=0),
        jnp.broadcast_to(log_gamma_row, (nl, GROUP)), zrows, zrows, zrows,
    ], axis=1).astype(F32)
    mu = _pad_to(rwkv_mu[:, None, :], (nl, 8, RWW))
    ln1 = _pad_to(jnp.stack([ln1_g, ln1_b], axis=1), (nl, 8, d))
    ln2 = _pad_to(jnp.stack([ln2_g, ln2_b], axis=1), (nl, 8, d))
    glaw2 = _hl(_pad_to(gla_gate_w2, (nl, 128, 128)))
    w2a2 = _hl(jnp.concatenate([_pad_to(rwkv_w2, (nl, 64, 512)),
                                jnp.pad(rwkv_a2, ((0, 0), (0, 0), (256, 0)))], axis=1))
    g2 = _pad_to(rwkv_g2, (nl, 256, 256)).astype(BF16)
    v1 = _pad_to(rwkv_v1, (nl - 1, 256, 128)).astype(BF16)
    v2 = _pad_to(rwkv_v2, (nl - 1, 128, 256)).astype(BF16)
    win = _pack_w_in(w_in)
    wout = w_out.astype(BF16)
    ffn_g, ffn_u, ffn_d = (w.astype(BF16) for w in (ffn_w_gate, ffn_w_up, ffn_w_down))
    moe_g, moe_u, moe_d = (w.astype(BF16) for w in (moe_w_gate, moe_w_up, moe_w_down))
    router = _hl(_pad_to(moe_router, (moe_router.shape[0], d, 128)))

    v_first = None
    for l in range(DEPTH):
        if l == 0:
            h, v_first = _mixer_call(l, h, cos_t, sin_t, win, wout, vecs, mu, ln1, glaw2, w2a2, g2,
                                     hs, rfac)
        else:
            h = _mixer_call(l, h, cos_t, sin_t, win, wout, vecs, mu, ln1, glaw2, w2a2, g2, hs, rfac,
                            v_first, v1, v2)
        h2 = h.reshape(b * s, d)
        if l % 2 == 0:
            h2 = _ffn_call(h2, ffn_g, ffn_u, ffn_d, ln2, l // 2, l)
        else:
            h2 = _moe_call(h2, router, moe_g, moe_u, moe_d, ln2, l // 2, l)
        h = h2.reshape(b, s, d)
    return h.astype(out_dtype)
```

```python
import functools

import jax
import jax.numpy as jnp
from jax import lax
from jax.experimental import pallas as pl
from jax.experimental.pallas import tpu as pltpu

F32 = jnp.float32
BF16 = jnp.bfloat16

D_MODEL = 1024
DEPTH = 4
GROUP = 256
NH = 4
HD = 64
CH = HD
GLA_KEY = 32
GLA_GATE_NORM = 16.0
RWKV_LNX_EPS = 64e-5
ROPE_BASE = 10000.0
FFN_DENSE = 2816
N_EXPERTS = 8
FFN_EXPERT = 1408
DN_ALPHA = (2.0 * DEPTH) ** 0.25
LN_EPS = 1e-5

HG0 = 0
GL0 = 1024
RW0 = 1920
RWW = 1152
RT0 = 3072
IN_PAD = 4096

TB_MIX = 128
TM_FFN = 512
FB_FFN = 256
TM_MOE = 1024
NSUB_MOE = 2
CSUB_MOE = 160
VMEM_LIMIT = 60 * 1024 * 1024


def _dot(a, b):
    return jnp.dot(a.astype(BF16), b.astype(BF16), preferred_element_type=F32)


def _dot_nt(a, b):
    return lax.dot_general(a.astype(BF16), b.astype(BF16), (((1,), (1,)), ((), ())),
                           preferred_element_type=F32)


def _dot_tn(a, b):
    return lax.dot_general(a.astype(BF16), b.astype(BF16), (((0,), (0,)), ((), ())),
                           preferred_element_type=F32)


def _split2(x):
    hi = x.astype(BF16)
    lo = (x - hi.astype(F32)).astype(BF16)
    return hi, lo


def _dot_hl(a, w_ref):
    a_hi, a_lo = _split2(a)
    w_hi = w_ref[0]
    w_lo = w_ref[1]
    return (jnp.dot(a_hi, w_hi, preferred_element_type=F32)
            + jnp.dot(a_lo, w_hi, preferred_element_type=F32)
            + jnp.dot(a_hi, w_lo, preferred_element_type=F32))


def _sigmoid(x):
    return 1.0 / (1.0 + jnp.exp(-x))


def _silu(x):
    return x * _sigmoid(x)


def _softplus(x):
    return jnp.maximum(x, 0.0) + jnp.log(1.0 + jnp.exp(-jnp.abs(x)))


def _layer_norm(x, g, b):
    mu = jnp.mean(x, axis=-1, keepdims=True)
    xc = x - mu
    var = jnp.mean(xc * xc, axis=-1, keepdims=True)
    return xc * lax.rsqrt(var + LN_EPS) * g + b


def _cumsum_rows(x, tri):
    hi, lo = _split2(x)
    return jnp.dot(tri, hi, preferred_element_type=F32) + jnp.dot(tri, lo, preferred_element_type=F32)


def _head_mean(x, hs):
    return jnp.dot(x.astype(BF16), hs, preferred_element_type=F32) * (1.0 / HD)


def _stack_heads(x, group):
    lane = lax.broadcasted_iota(jnp.int32, (1, x.shape[1]), 1)
    parts = [jnp.where((lane // group) == h, x, 0.0) for h in range(NH)]
    return jnp.concatenate(parts, axis=0)


def _run_interleaved(gens):
    alive = list(gens)
    while alive:
        still = []
        for g in alive:
            try:
                next(g)
                still.append(g)
            except StopIteration:
                pass
        alive = still


def _lockstep(gens):
    results = [None] * len(gens)
    alive = list(range(len(gens)))
    while alive:
        still = []
        for i in alive:
            try:
                next(gens[i])
                still.append(i)
            except StopIteration as stop:
                results[i] = stop.value
        alive = still
        if alive:
            yield
    return results


def _gla_core(q, k, v, st_ref, kgroup, tri, logf=None, factors=None):
    c = q.shape[0]
    kt = q.shape[1]
    if factors is None:
        cum = _cumsum_rows(logf, tri)
        yield
        mid = cum[c // 2:c // 2 + 1]
        cend = cum[c - 1:c]
        e_i = jnp.exp(cum)
        e_a = jnp.exp(cum - mid)
        e_k = jnp.exp(mid - cum)
        e_s = jnp.exp(cend - cum)
        e_end = jnp.exp(cend)
    else:
        e_i, e_a, e_k, e_s, e_end = factors
    a = _dot_nt(q * e_a, _stack_heads(k * e_k, kgroup))
    st = st_ref[...]
    o_int = _dot_nt(q * e_i, st)
    upd = _dot_tn(v, k * e_s)
    yield
    row = lax.broadcasted_iota(jnp.int32, a.shape, 0)
    col = lax.broadcasted_iota(jnp.int32, a.shape, 1) & (c - 1)
    a = jnp.where(row >= col, a, 0.0).astype(BF16)
    r_h = lax.broadcasted_iota(jnp.int32, (NH * HD, kt), 0) // HD
    c_h = lax.broadcasted_iota(jnp.int32, (NH * HD, kt), 1) // kgroup
    st_ref[...] = st * e_end + jnp.where(r_h == c_h, upd, 0.0)
    o_intra = jnp.dot(a, _stack_heads(v, HD).astype(BF16), preferred_element_type=F32)
    yield
    return o_int + o_intra


def _rwkv_core(r, logw, k, v, kk, a, st_ref, tri):
    c = r.shape[0]
    n = NH * c
    cum = _cumsum_rows(logw, tri)
    yield
    cump = cum - logw
    mid = cum[c // 2:c // 2 + 1]
    cend = cum[c - 1:c]
    eq = jnp.exp(cum - mid)
    eqp = jnp.exp(cump - mid)
    ek = jnp.exp(mid - cum)
    es = jnp.exp(cend - cum)
    ei = jnp.exp(cum)
    eip = jnp.exp(cump)
    b = kk * a
    na = -kk

    def stack(x):
        return _stack_heads(x, HD).astype(BF16)

    lhs = jnp.concatenate([na * eqp, r * eq], axis=0).astype(BF16)
    rhs = jnp.concatenate([stack(b * ek), stack(k * ek)], axis=0)
    sc = lax.dot_general(lhs, rhs, (((1,), (1,)), ((), ())), preferred_element_type=F32)
    st = st_ref[...]
    stb = st.astype(BF16)
    rs = _dot_nt(jnp.concatenate([na * eip, r * ei], axis=0), stb)
    rs_u = rs[:c]
    rs_o = rs[c:]
    vs = stack(v)
    yield
    i0 = lax.broadcasted_iota(jnp.int32, (c, n), 0)
    i1 = lax.broadcasted_iota(jnp.int32, (c, n), 1) & (c - 1)
    strict = i0 > i1
    incl = i0 >= i1
    a_ab = jnp.where(strict, sc[:c, :n], 0.0)
    a_ak = jnp.where(strict, sc[:c, n:], 0.0).astype(BF16)
    a_rb = jnp.where(incl, sc[c:, :n], 0.0).astype(BF16)
    a_rk = jnp.where(incl, sc[c:, n:], 0.0).astype(BF16)
    same16 = (i0 >> 4) == (i1 >> 4)
    eye = jnp.where(i0 == i1, 1.0, 0.0).astype(F32)
    dm = jnp.where(same16, a_ab, 0.0)
    em = a_ab - dm
    p = eye + dm
    d2 = _dot(dm, stack(dm))
    av = jnp.dot(jnp.concatenate([a_ak, a_rk], axis=0), vs, preferred_element_type=F32)
    ak_v = av[:c]
    rk_v = av[c:]
    yield
    pd = _dot(jnp.concatenate([p, d2], axis=0), stack(d2))
    yield
    p = p + pd[:c]
    d4 = pd[c:]
    pd = _dot(jnp.concatenate([p, d4], axis=0), stack(d4))
    yield
    p = p + pd[:c]
    d8 = pd[c:]
    pd = _dot(p, stack(d8))
    yield
    p = p + pd
    f = _dot(p, stack(em))
    yield
    f2 = _dot(f, stack(f))
    yield
    ff2 = _dot(f, stack(f2))
    yield
    g = eye + f + f2 + ff2
    t = _dot(g, stack(p))
    yield
    u = _dot(t, stack(rs_u + ak_v))
    yield
    rb_u = jnp.dot(a_rb, stack(u), preferred_element_type=F32)
    s_new = _dot_tn(jnp.concatenate([u, v], axis=0), jnp.concatenate([b * es, k * es], axis=0))
    yield
    r_h = lax.broadcasted_iota(jnp.int32, s_new.shape, 0) // HD
    c_h = lax.broadcasted_iota(jnp.int32, s_new.shape, 1) // HD
    st_ref[...] = st * jnp.exp(cend) + jnp.where(r_h == c_h, s_new, 0.0)
    return rs_o + rb_u + rk_v


def _mixer_kernel(first, nb, tb, *refs):
    if first:
        (h_ref, hn_ref, cos_ref, sin_ref, win_ref, wout_ref, vec_ref, mu_ref, ln_ref, glaw2_ref, w2a2_ref,
         g2_ref, hs_ref, rfac_ref, out_ref, vf_out_ref, p_scr, hbn_scr, y_scr, st_hg, st_gl, st_rw,
         st_rt) = refs
        vf_in_ref = v1_ref = v2_ref = None
    else:
        (h_ref, hn_ref, cos_ref, sin_ref, win_ref, wout_ref, vec_ref, mu_ref, ln_ref, glaw2_ref, w2a2_ref,
         g2_ref, hs_ref, rfac_ref, vf_in_ref, v1_ref, v2_ref, out_ref, p_scr, hbn_scr, y_scr, st_hg,
         st_gl, st_rw, st_rt) = refs
        vf_out_ref = None
    c = CH
    t_idx = pl.program_id(0)

    slot = lax.rem(t_idx, 2)
    cur = p_scr.at[slot]
    nxt = p_scr.at[1 - slot]
    pw = 256

    def project(hb, dst, j):
        res = jnp.dot(hb, win_ref[:, j * pw:(j + 1) * pw], preferred_element_type=F32)
        for bi in range(nb):
            dst[bi, 8:tb + 8, j * pw:(j + 1) * pw] = res[bi * tb:(bi + 1) * tb]

    @pl.when(t_idx == 0)
    def _():
        st_hg[...] = jnp.zeros_like(st_hg)
        st_gl[...] = jnp.zeros_like(st_gl)
        st_rw[...] = jnp.zeros_like(st_rw)
        st_rt[...] = jnp.zeros_like(st_rt)
        cur[:, 0:8, RW0:RW0 + RWW] = jnp.zeros((nb, 8, RWW), F32)
        hb0 = h_ref[...].reshape(nb * tb, D_MODEL).astype(BF16)
        for j in range(IN_PAD // pw):
            project(hb0, cur, j)

    hbn_scr[...] = hn_ref[...].reshape(nb * tb, D_MODEL).astype(BF16)
    n_chunks = tb // c
    per_chunk = IN_PAD // pw // n_chunks

    def project_next(ci):
        for j in range(ci * per_chunk, (ci + 1) * per_chunk):
            project(hbn_scr[...], nxt, j)
            yield

    def vec(i, n=GROUP):
        return vec_ref[i:i + 1, 0:n]

    def chunk_body(ci, carry):
        r0 = ci * c
        rows = pl.ds(r0 + 8, c)
        crow = pl.ds(r0, c)
        ti = lax.broadcasted_iota(jnp.int32, (c, c), 0)
        tj = lax.broadcasted_iota(jnp.int32, (c, c), 1)
        tri = jnp.where(ti >= tj, 1.0, 0.0).astype(BF16)

        def ld(c0, c1):
            return jnp.concatenate([cur[bi, rows, c0:c1] for bi in range(nb)], axis=0)

        def sl(x, bi):
            return x[bi * c:(bi + 1) * c]

        def store_y(c0, val):
            for bi in range(nb):
                y_scr[pl.ds(bi * tb + r0, c), c0:c0 + GROUP] = sl(val, bi).astype(BF16)

        def hgrn():
            q = _silu(ld(HG0, HG0 + 256))
            lb = vec(0)
            fg = lb + (1.0 - lb) * _sigmoid(ld(HG0 + 256, HG0 + 512))
            k = 1.0 - fg
            logf = jnp.log(fg)
            v = ld(HG0 + 512, HG0 + 768)
            outs = yield from _lockstep([
                _gla_core(sl(q, bi), sl(k, bi), sl(v, bi), st_hg.at[bi], HD, tri, logf=sl(logf, bi))
                for bi in range(nb)])
            o = jnp.concatenate(outs, axis=0)
            ms = _head_mean(o * o, hs_ref[...])
            yield
            o = o * lax.rsqrt(ms + 1e-6) * vec(1) * _silu(ld(HG0 + 768, HG0 + 1024))
            store_y(0, o)

        def gla():
            z = _dot_hl(ld(GL0 + 768, GL0 + 896), glaw2_ref)
            yield
            logf = -_softplus(-(z + vec(2, 128))) * (1.0 / GLA_GATE_NORM)
            q = ld(GL0, GL0 + 128) * (GLA_KEY ** -0.5)
            k = ld(GL0 + 128, GL0 + 256)
            v = ld(GL0 + 256, GL0 + 512)
            outs = yield from _lockstep([
                _gla_core(sl(q, bi), sl(k, bi), sl(v, bi), st_gl.at[bi], GLA_KEY, tri,
                          logf=sl(logf, bi))
                for bi in range(nb)])
            o = jnp.concatenate(outs, axis=0)
            ms = _head_mean(o * o, hs_ref[...])
            yield
            o = o * lax.rsqrt(ms + 1e-6) * vec(3) * _silu(ld(GL0 + 512, GL0 + 768))
            store_y(256, o)

        def rwkv():
            hs = hs_ref[...]
            curs, prevs = [], []
            for bi in range(nb):
                xs = cur[bi, pl.ds(r0, c + 8), RW0:RW0 + RWW]
                curs.append(xs[8:])
                prevs.append(pltpu.roll(xs, 1, 0)[8:])
            now = jnp.concatenate(curs, axis=0)
            prev = jnp.concatenate(prevs, axis=0)
            pm = now + (prev - now) * mu_ref[0:1, :]
            r = pm[:, 0:256]
            k = pm[:, 256:512]
            v = pm[:, 512:768]
            wa = pm[:, 768:896]
            lane = lax.broadcasted_iota(jnp.int32, (1, 128), 1)
            pre = _dot_hl(jnp.where(lane < 64, jnp.tanh(wa), wa), w2a2_ref)
            gate = _dot(_sigmoid(pm[:, 896:1152]), g2_ref[...])
            kk = k * vec(6)
            kk_ms = _head_mean(kk * kk, hs)
            if first:
                for bi in range(nb):
                    vf_out_ref[bi, crow, :] = sl(v, bi)
            else:
                t1 = _dot(v, v1_ref[...])
            yield
            w = -_softplus(-(vec(4) + pre[:, 0:256])) - 0.5
            logw = -jnp.exp(w)
            a = _sigmoid(vec(5) + pre[:, 256:512])
            kk = kk / jnp.maximum(jnp.sqrt(kk_ms * HD), 1e-12)
            k = k * (1.0 + (a - 1.0) * vec(7))
            rk = _head_mean(r * k * vec(8), hs)
            if not first:
                t2 = _dot(t1, v2_ref[...])
                yield
                vf = jnp.concatenate([vf_in_ref[bi, crow, :] for bi in range(nb)], axis=0)
                v = v + (vf - v) * _sigmoid(vec(11) + t2)
            outs = yield from _lockstep([
                _rwkv_core(sl(r, bi), sl(logw, bi), sl(k, bi), sl(v, bi), sl(kk, bi), sl(a, bi),
                           st_rw.at[bi], tri)
                for bi in range(nb)])
            o = jnp.concatenate(outs, axis=0)
            mu_o = _head_mean(o, hs)
            yield
            oc = o - mu_o
            var = _head_mean(oc * oc, hs)
            yield
            o = oc * lax.rsqrt(var + RWKV_LNX_EPS) * vec(9) + vec(10)
            o = o + (rk * HD) * v
            store_y(512, o * gate)

        def retnet():
            hs = hs_ref[...]
            q = ld(RT0, RT0 + 256)
            k = ld(RT0 + 256, RT0 + 512)
            v = ld(RT0 + 512, RT0 + 768)
            cs = jnp.concatenate([cos_ref[bi, crow, :] for bi in range(nb)], axis=0)
            sn = jnp.concatenate([sin_ref[bi, crow, :] for bi in range(nb)], axis=0)
            cs = jnp.concatenate([cs, cs], axis=1)
            sn = jnp.concatenate([sn, sn], axis=1)
            lane_r = lax.broadcasted_iota(jnp.int32, (1, 256), 1)
            first_half = (lane_r & (HD - 1)) < (HD // 2)

            def rot(x):
                swapped = jnp.where(first_half, pltpu.roll(x, 256 - HD // 2, 1),
                                    pltpu.roll(x, HD // 2, 1))
                return x * cs + swapped * sn

            q = rot(q)
            k = rot(k) * (HD ** -0.5)
            factors = (rfac_ref[0], rfac_ref[1], rfac_ref[2], rfac_ref[3], rfac_ref[4, 0:1, :])
            outs = yield from _lockstep([
                _gla_core(sl(q, bi), sl(k, bi), sl(v, bi), st_rt.at[bi], HD, tri, factors=factors)
                for bi in range(nb)])
            o = jnp.concatenate(outs, axis=0)
            mu_o = _head_mean(o, hs)
            yield
            oc = o - mu_o
            var = _head_mean(oc * oc, hs)
            yield
            store_y(768, _silu(ld(RT0 + 768, RT0 + 1024)) * oc * lax.rsqrt(var + LN_EPS))

        _run_interleaved([rwkv(), hgrn(), gla(), retnet(), project_next(ci)])
        return carry

    for ci in range(n_chunks):
        chunk_body(ci, 0)
    nxt[:, 0:8, RW0:RW0 + RWW] = cur[:, tb:tb + 8, RW0:RW0 + RWW]

    mix = jnp.dot(y_scr[...], wout_ref[...], preferred_element_type=F32)
    hres = DN_ALPHA * h_ref[...].reshape(nb * tb, D_MODEL) + mix
    out_ref[...] = _layer_norm(hres, ln_ref[0:1, :], ln_ref[1:2, :]).reshape(nb, tb, D_MODEL)


def _mixer_call(layer, h, cos_t, sin_t, w_in, w_out, vecs, mu, ln, glaw2, w2a2, g2, hs, rfac,
                vf=None, v1=None, v2=None):
    first = layer == 0
    b, s, d = h.shape
    tb = TB_MIX
    nb = b
    grid = (s // tb,)

    def tok(width):
        return pl.BlockSpec((nb, tb, width), lambda j: (0, j, 0))

    def full(arr):
        nd = arr.ndim
        return pl.BlockSpec(arr.shape, lambda j: (0,) * nd)

    def of_layer(arr, li):
        nd = arr.ndim
        return pl.BlockSpec((None,) + arr.shape[1:], lambda j: (li,) + (0,) * (nd - 1))

    last = grid[0] - 1
    tok_next = pl.BlockSpec((nb, tb, d), lambda j: (0, jnp.minimum(j + 1, last), 0))
    per_layer = [w_in, w_out, vecs, mu, ln, glaw2, w2a2, g2]
    args = [h, h, cos_t, sin_t] + per_layer + [hs, rfac]
    in_specs = ([tok(d), tok_next, tok(128), tok(128)] + [of_layer(a, layer) for a in per_layer]
                + [full(hs), full(rfac)])
    out_h = jax.ShapeDtypeStruct((b, s, d), F32)
    if first:
        out_shape = (out_h, jax.ShapeDtypeStruct((b, s, GROUP), F32))
        out_specs = (tok(d), tok(GROUP))
    else:
        args += [vf, v1, v2]
        in_specs += [tok(GROUP), of_layer(v1, layer - 1), of_layer(v2, layer - 1)]
        out_shape = out_h
        out_specs = tok(d)
    scratch = [
        pltpu.VMEM((2, nb, tb + 8, IN_PAD), F32),
        pltpu.VMEM((nb * tb, d), BF16),
        pltpu.VMEM((nb * tb, d), BF16),
        pltpu.VMEM((nb, GROUP, GROUP), F32),
        pltpu.VMEM((nb, GROUP, NH * GLA_KEY), F32),
        pltpu.VMEM((nb, GROUP, GROUP), F32),
        pltpu.VMEM((nb, GROUP, GROUP), F32),
    ]
    return pl.pallas_call(
        functools.partial(_mixer_kernel, first, nb, tb),
        grid=grid,
        in_specs=in_specs,
        out_specs=out_specs,
        out_shape=out_shape,
        scratch_shapes=scratch,
        compiler_params=pltpu.CompilerParams(
            dimension_semantics=("arbitrary",), vmem_limit_bytes=VMEM_LIMIT),
        name="token_mixer_first" if first else "token_mixer",
    )(*args)


def _ffn_kernel(h_ref, wg_ref, wu_ref, wd_ref, ln_ref, out_ref, acc_scr):
    hb = h_ref[...].astype(BF16)
    fb = FB_FFN
    for k in range(wg_ref.shape[1] // fb):
        cols = slice(k * fb, (k + 1) * fb)
        g = jnp.dot(hb, wg_ref[:, cols], preferred_element_type=F32)
        u = jnp.dot(hb, wu_ref[:, cols], preferred_element_type=F32)
        act = (_silu(g) * u).astype(BF16)
        y = jnp.dot(act, wd_ref[cols, :], preferred_element_type=F32)
        if k == 0:
            acc_scr[...] = y
        else:
            acc_scr[...] += y
    out_ref[...] = _layer_norm(DN_ALPHA * h_ref[...] + acc_scr[...], ln_ref[0:1, :], ln_ref[1:2, :])


def _ffn_call(h2, wg, wu, wd, ln, li, layer):
    t, d = h2.shape
    f = wg.shape[2]
    tm = TM_FFN
    return pl.pallas_call(
        _ffn_kernel,
        grid=(t // tm,),
        in_specs=[
            pl.BlockSpec((tm, d), lambda i: (i, 0)),
            pl.BlockSpec((None, d, f), lambda i: (li, 0, 0)),
            pl.BlockSpec((None, d, f), lambda i: (li, 0, 0)),
            pl.BlockSpec((None, f, d), lambda i: (li, 0, 0)),
            pl.BlockSpec((None,) + ln.shape[1:], lambda i: (layer, 0, 0)),
        ],
        out_specs=pl.BlockSpec((tm, d), lambda i: (i, 0)),
        out_shape=jax.ShapeDtypeStruct((t, d), F32),
        scratch_shapes=[pltpu.VMEM((tm, d), F32)],
        compiler_params=pltpu.CompilerParams(
            dimension_semantics=("arbitrary",), vmem_limit_bytes=VMEM_LIMIT),
        name="ffn_dense",
    )(h2, wg, wu, wd, ln)


def _moe_kernel(h_ref, rt_ref, tri_ref, wg_ref, wu_ref, wd_ref, ln_ref, out_ref,
                hb_scr, row_scr, acc_scr):
    e = pl.program_id(1)
    nsub = NSUB_MOE
    sub = h_ref.shape[0] // nsub
    csub = CSUB_MOE
    ne = N_EXPERTS

    @pl.when(e == 0)
    def _():
        h = h_ref[...]
        hb_scr[...] = h.astype(BF16)
        acc_scr[...] = jnp.zeros_like(acc_scr)
        logits = _dot_hl(h, rt_ref)
        lane = lax.broadcasted_iota(jnp.int32, logits.shape, 1)
        neg = jnp.float32(-jnp.inf)
        logits = jnp.where(lane < ne, logits, neg)
        m1 = jnp.max(logits, axis=-1, keepdims=True)
        i1 = jnp.min(jnp.where(logits == m1, lane, 128), axis=-1, keepdims=True)
        rest = jnp.where(lane == i1, neg, logits)
        m2 = jnp.max(rest, axis=-1, keepdims=True)
        i2 = jnp.min(jnp.where(rest == m2, lane, 128), axis=-1, keepdims=True)
        e2 = jnp.exp(m2 - m1)
        g1 = 1.0 / (1.0 + e2)
        g2 = e2 / (1.0 + e2)
        sel = (lane == i1) | (lane == i2)
        gates = jnp.where(lane == i1, g1, 0.0) + jnp.where(lane == i2, g2, 0.0)
        selb = jnp.where(sel, 1.0, 0.0).astype(BF16)
        tri = tri_ref[...]
        rank = jnp.concatenate(
            [jnp.dot(tri, selb[j * sub:(j + 1) * sub], preferred_element_type=F32) for j in range(nsub)],
            axis=0)
        slot = jnp.where(sel, rank, -1.0)
        packed = jnp.where(lane < ne, gates, pltpu.roll(slot, ne, 1))
        row_scr[...] = packed.T

    slot_row = row_scr[pl.ds(ne + e, 1), :]
    hit = jnp.where(slot_row >= 0.0, 1.0, 0.0)
    n_max = jnp.sum(hit[:, 0:sub])
    for j in range(1, nsub):
        n_max = jnp.maximum(n_max, jnp.sum(hit[:, j * sub:(j + 1) * sub]))
    n_tiles = (n_max.astype(jnp.int32) + (csub - 1)) // csub
    gate_row = row_scr[pl.ds(e, 1), :]

    def tile(s, carry):
        base = (s * csub).astype(F32)
        jr = lax.broadcasted_iota(jnp.int32, (csub, sub), 0).astype(F32) + base
        xs, hits = [], []
        for j in range(nsub):
            hits.append(jr == slot_row[:, j * sub:(j + 1) * sub])
            xs.append(jnp.dot(jnp.where(hits[j], 1.0, 0.0).astype(BF16), hb_scr[j * sub:(j + 1) * sub, :],
                              preferred_element_type=F32).astype(BF16))
        x = jnp.concatenate(xs, axis=0)
        g = jnp.dot(x, wg_ref[...], preferred_element_type=F32)
        u = jnp.dot(x, wu_ref[...], preferred_element_type=F32)
        act = (_silu(g) * u).astype(BF16)
        y = jnp.dot(act, wd_ref[...], preferred_element_type=F32).astype(BF16)
        for j in range(nsub):
            rows = slice(j * sub, (j + 1) * sub)
            weighted = jnp.where(hits[j], gate_row[:, rows], 0.0)
            acc_scr[rows, :] += _dot_tn(weighted, y[j * csub:(j + 1) * csub])
        return carry

    lax.fori_loop(0, n_tiles, tile, 0)

    @pl.when(e == pl.num_programs(1) - 1)
    def _():
        out_ref[...] = _layer_norm(DN_ALPHA * h_ref[...] + acc_scr[...], ln_ref[0:1, :], ln_ref[1:2, :])


def _moe_call(h2, router_hl, wg, wu, wd, ln, li, layer):
    t, d = h2.shape
    _, ne, _, f = wg.shape
    tm = TM_MOE
    sub = tm // NSUB_MOE
    ti = jnp.arange(sub)
    tri = (ti[:, None] > ti[None, :]).astype(BF16)
    return pl.pallas_call(
        _moe_kernel,
        grid=(t // tm, ne),
        in_specs=[
            pl.BlockSpec((tm, d), lambda i, e: (i, 0)),
            pl.BlockSpec((None,) + router_hl.shape[1:], lambda i, e: (li, 0, 0, 0)),
            pl.BlockSpec((sub, sub), lambda i, e: (0, 0)),
            pl.BlockSpec((None, None, d, f), lambda i, e: (li, e, 0, 0)),
            pl.BlockSpec((None, None, d, f), lambda i, e: (li, e, 0, 0)),
            pl.BlockSpec((None, None, f, d), lambda i, e: (li, e, 0, 0)),
            pl.BlockSpec((None,) + ln.shape[1:], lambda i, e: (layer, 0, 0)),
        ],
        out_specs=pl.BlockSpec((tm, d), lambda i, e: (i, 0)),
        out_shape=jax.ShapeDtypeStruct((t, d), F32),
        scratch_shapes=[pltpu.VMEM((tm, d), BF16),
                        pltpu.VMEM((128, tm), F32), pltpu.VMEM((tm, d), F32)],
        compiler_params=pltpu.CompilerParams(
            dimension_semantics=("arbitrary", "arbitrary"), vmem_limit_bytes=VMEM_LIMIT),
        name="ffn_experts",
    )(h2, router_hl, tri, wg, wu, wd, ln)


def _trig_kernel(ang_ref, cos_ref, sin_ref):
    ang = ang_ref[...]
    cos_ref[...] = jnp.cos(ang)
    sin_ref[...] = jnp.sin(ang)


def _rotary_tables(positions):
    b, s = positions.shape
    half = HD // 2
    inv_freq = ROPE_BASE ** (-jnp.arange(half, dtype=F32) / half)
    ang = positions.astype(F32)[:, :, None] * inv_freq
    rows = b * s * half // 128
    ang2 = ang.reshape(rows, 128)
    blk = min(rows, 1024)
    cos2, sin2 = pl.pallas_call(
        _trig_kernel,
        grid=(rows // blk,),
        in_specs=[pl.BlockSpec((blk, 128), lambda i: (i, 0))],
        out_specs=(pl.BlockSpec((blk, 128), lambda i: (i, 0)),) * 2,
        out_shape=(jax.ShapeDtypeStruct((rows, 128), F32),) * 2,
        name="rotary_tables",
    )(ang2)
    cos = cos2.reshape(b, s, half)
    sin = sin2.reshape(b, s, half)
    cos_t = jnp.concatenate([cos, cos, cos, cos], axis=-1)
    sin_t = jnp.concatenate([-sin, sin, -sin, sin], axis=-1)
    return cos_t, sin_t


def _hl(w):
    hi = w.astype(BF16)
    lo = (w - hi.astype(F32)).astype(BF16)
    return jnp.stack([hi, lo], axis=1)


def _pad_to(a, shape):
    return jnp.pad(a, [(0, t - s) for s, t in zip(a.shape, shape)])


def _pack_w_in(w):
    w = w.astype(BF16)
    z = lambda n: jnp.zeros(w.shape[:2] + (n,), BF16)
    gl, rw, rt = 1024, 1808, 2864
    cols = [
        w[..., 0:1024],
        w[..., gl:gl + 512], w[..., gl + 528:gl + 784], w[..., gl + 512:gl + 528], z(112),
        w[..., rw:rw + 1056], z(96),
        w[..., rt:rt + 1024],
    ]
    return jnp.concatenate(cols, axis=-1)


def kernel(x, positions, w_in, w_out, ln1_g, ln1_b, ln2_g, ln2_b, hgrn_lb_logits, hgrn_norm_g,
           gla_gate_w2, gla_gate_b, gla_norm_g, rwkv_mu, rwkv_w0, rwkv_w2, rwkv_a0, rwkv_a2,
           rwkv_g2, rwkv_k_k, rwkv_k_a, rwkv_r_k, rwkv_lnx_g, rwkv_lnx_b, rwkv_v0, rwkv_v1,
           rwkv_v2, ffn_w_gate, ffn_w_up, ffn_w_down, moe_router, moe_w_gate, moe_w_up,
           moe_w_down):
    out_dtype = x.dtype
    b, s, d = x.shape
    h = x.astype(F32)
    cos_t, sin_t = _rotary_tables(positions)

    lb_all = jnp.cumsum(jax.nn.softmax(hgrn_lb_logits.astype(F32), axis=0), axis=0)
    lb_all = lb_all - lb_all[:1]
    log_gamma = jnp.log(1.0 - 2.0 ** (-5.0 - jnp.arange(NH, dtype=F32)))
    log_gamma_row = jnp.repeat(log_gamma, HD)
    head_i = jnp.arange(GROUP) // HD
    hs = (head_i[:, None] == head_i[None, :]).astype(BF16)
    cum = (jnp.arange(CH, dtype=F32)[:, None] + 1.0) * log_gamma_row[None, :]
    mid, cend = cum[CH // 2:CH // 2 + 1], cum[CH - 1:CH]
    rfac = jnp.stack([jnp.exp(cum), jnp.exp(cum - mid), jnp.exp(mid - cum), jnp.exp(cend - cum),
                      jnp.broadcast_to(jnp.exp(cend), cum.shape)])

    nl = DEPTH
    tile4 = lambda v: jnp.tile(v, (1, NH))
    zrows = jnp.zeros((nl, GROUP), F32)
    vecs = jnp.stack([
        lb_all, tile4(hgrn_norm_g), _pad_to(gla_gate_b, (nl, GROUP)), tile4(gla_norm_g),
        rwkv_w0, rwkv_a0, rwkv_k_k, rwkv_k_a, rwkv_r_k.reshape(nl, GROUP),
        rwkv_lnx_g, rwkv_lnx_b, jnp.concatenate([zrows[:1], rwkv_v0], axis=0),
        jnp.broadcast_to(log_gamma_row, (nl, GROUP)), zrows, zrows, zrows,
    ], axis=1).astype(F32)
    mu = _pad_to(rwkv_mu[:, None, :], (nl, 8, RWW))
    ln1 = _pad_to(jnp.stack([ln1_g, ln1_b], axis=1), (nl, 8, d))
    ln2 = _pad_to(jnp.stack([ln2_g, ln2_b], axis=1), (nl, 8, d))
    glaw2 = _hl(_pad_to(gla_gate_w2, (nl, 128, 128)))
    w2a2 = _hl(jnp.concatenate([_pad_to(rwkv_w2, (nl, 64, 512)),
                                jnp.pad(rwkv_a2, ((0, 0), (0, 0), (256, 0)))], axis=1))
    g2 = _pad_to(rwkv_g2, (nl, 256, 256)).astype(BF16)
    v1 = _pad_to(rwkv_v1, (nl - 1, 256, 128)).astype(BF16)
    v2 = _pad_to(rwkv_v2, (nl - 1, 128, 256)).astype(BF16)
    win = _pack_w_in(w_in)
    wout = w_out.astype(BF16)
    ffn_g, ffn_u, ffn_d = (w.astype(BF16) for w in (ffn_w_gate, ffn_w_up, ffn_w_down))
    moe_g, moe_u, moe_d = (w.astype(BF16) for w in (moe_w_gate, moe_w_up, moe_w_down))
    router = _hl(_pad_to(moe_router, (moe_router.shape[0], d, 128)))

    v_first = None
    for l in range(DEPTH):
        if l == 0:
            h, v_first = _mixer_call(l, h, cos_t, sin_t, win, wout, vecs, mu, ln1, glaw2, w2a2, g2,
                                     hs, rfac)
        else:
            h = _mixer_call(l, h, cos_t, sin_t, win, wout, vecs, mu, ln1, glaw2, w2a2, g2, hs, rfac,
                            v_first, v1, v2)
        h2 = h.reshape(b * s, d)
        if l % 2 == 0:
            h2 = _ffn_call(h2, ffn_g, ffn_u, ffn_d, ln2, l // 2, l)
        else:
            h2 = _moe_call(h2, router, moe_g, moe_u, moe_d, ln2, l // 2, l)
        h = h2.reshape(b, s, d)
    return h.astype(out_dtype)
```
